```python
import math
import jax, jax.numpy as jnp
from jax import lax
import numpy as np

D_MODEL = 1024
BATCH = 2
SEQ = 8192
DEPTH = 1
DEC_BATCH = 32
DEC_SEQ = 8
PAST_LEN = 8192
PAGE_SIZE = 128

HEAD_DIM = 64
N_SB_HEADS = 8
N_DIFF_HEADS = 8
N_HEADS = N_SB_HEADS + N_DIFF_HEADS
MIX_WIDTH = N_HEADS * HEAD_DIM
DIFF_QK_DIM = HEAD_DIM // 2
D_FF = 2816
CONV_WIDTH = 3
Q_BLOCK = 128
NORM_EPS = 1e-6
SUBLN_EPS = 1e-5
N_MOD = 6

kernel_name = 'sb_diff_hybrid_decoder_step'


def _rmsnorm(x, g, eps=NORM_EPS):
    xf = x.astype(jnp.float32)
    y = xf * lax.rsqrt(jnp.mean(xf * xf, axis=-1, keepdims=True) + eps) * g.astype(jnp.float32)
    return y.astype(x.dtype)


def _alibi_slopes():
    return 2.0 ** (-(8.0 / N_DIFF_HEADS) * jnp.arange(1, N_DIFF_HEADS + 1, dtype=jnp.float32))


def _stick_breaking(q, k, v, q_pos, k_pos):
    z = jnp.einsum('bqhd,bkhd->bhqk', q, k).astype(jnp.float32) * (HEAD_DIM ** -0.5)
    mask = k_pos[None, :] < q_pos[:, None]
    log_not = jnp.where(mask, -jax.nn.softplus(z), 0.0)
    between = lax.cumsum(log_not, axis=3, reverse=True) - log_not
    w = jnp.where(mask, jnp.exp(jax.nn.log_sigmoid(z) + between), 0.0)
    return jnp.einsum('bhqk,bkhd->bqhd', w.astype(v.dtype), v)


def _diff_attention(q, k, v, q_pos, k_pos, lam, lambda_init, g_subln):
    B, Tq, H, _ = q.shape
    Tk = k.shape[1]
    q = q.reshape(B, Tq, H, 2, DIFF_QK_DIM)
    k = k.reshape(B, Tk, H, 2, DIFF_QK_DIM)
    s = jnp.einsum('bqhcd,bkhcd->bchqk', q, k).astype(jnp.float32) * (DIFF_QK_DIM ** -0.5)
    dist = (q_pos[:, None] - k_pos[None, :]).astype(jnp.float32)
    bias = -_alibi_slopes()[:, None, None] * dist
    mask = k_pos[None, :] <= q_pos[:, None]
    s = jnp.where(mask, s + bias, -jnp.inf)
    p = jax.nn.softmax(s, axis=-1)
    a = p[:, 0] - lam * p[:, 1]
    o = jnp.einsum('bhqk,bkhd->bqhd', a.astype(v.dtype), v)
    return _rmsnorm(o, g_subln, SUBLN_EPS) * (1.0 - lambda_init)


def _mixers(q, k, v, q_pos, k_pos, lam, lambda_init, g_subln):
    sb = _stick_breaking(q[:, :, :N_SB_HEADS], k[:, :, :N_SB_HEADS], v[:, :, :N_SB_HEADS], q_pos, k_pos)
    df = _diff_attention(q[:, :, N_SB_HEADS:], k[:, :, N_SB_HEADS:], v[:, :, N_SB_HEADS:],
                         q_pos, k_pos, lam, lambda_init, g_subln)
    o = jnp.concatenate([sb, df], axis=2)
    return o.reshape(o.shape[0], o.shape[1], MIX_WIDTH)


def _layer(x, c, conv_prev, attend, w_ada, b_ada, g_pre_attn, g_post_attn, w_in, w_out,
           g_pre_mlp, g_post_mlp, w_up, conv_w, conv_b, w_down):
    B, T, _ = x.shape
    mod = jnp.dot(jax.nn.silu(c), w_ada) + b_ada
    sh_a, sc_a, gt_a, sh_m, sc_m, gt_m = jnp.split(mod[:, None, :], N_MOD, axis=-1)
    h = _rmsnorm(x, g_pre_attn) * (1.0 + sc_a) + sh_a
    qkv = jnp.dot(h, w_in).reshape(B, T, 3, N_HEADS, HEAD_DIM)
    q, k, v = qkv[:, :, 0], qkv[:, :, 1], qkv[:, :, 2]
    mixed = attend(q, k, v)
    x = x + gt_a * _rmsnorm(jnp.dot(mixed, w_out), g_post_attn)
    h = _rmsnorm(x, g_pre_mlp) * (1.0 + sc_m) + sh_m
    u = jnp.dot(h, w_up)
    u_full = jnp.concatenate([conv_prev, u], axis=1)
    conv = conv_b + sum(conv_w[i] * u_full[:, i:i + T] for i in range(CONV_WIDTH))
    a, b = jnp.split(conv, 2, axis=-1)
    f = jnp.dot(jax.nn.silu(a) * b, w_down)
    x = x + gt_m * _rmsnorm(f, g_post_mlp)
    return x, k, v, u_full[:, -(CONV_WIDTH - 1):]


def setup_inputs(seed: int = 0) -> dict:
    key = jax.random.key(seed)
    ks = jax.random.split(key, 24)
    n_pages = PAST_LEN // PAGE_SIZE
    n_pool = (DEC_BATCH * n_pages * 5 + 3) // 4
    f32 = jnp.float32
    nrm = lambda k, s, sc: jax.random.normal(k, s, f32) * sc
    gain = lambda k, s: 1.0 + 0.01 * jax.random.normal(k, s, f32)
    page_table = jax.random.permutation(ks[7], n_pool)[:DEC_BATCH * n_pages]
    return {
        'x_prompt': nrm(ks[0], (BATCH, SEQ, D_MODEL), 1.0),
        'x_sample': nrm(ks[1], (DEC_BATCH, DEC_SEQ, D_MODEL), 1.0),
        'c_prompt': nrm(ks[2], (BATCH, D_MODEL), 1.0),
        'c_sample': nrm(ks[3], (DEC_BATCH, D_MODEL), 1.0),
        'cache_k': nrm(ks[4], (DEPTH, n_pool, PAGE_SIZE, N_HEADS, HEAD_DIM), 1.0),
        'cache_v': nrm(ks[5], (DEPTH, n_pool, PAGE_SIZE, N_HEADS, HEAD_DIM), 1.0),
        'state_conv': nrm(ks[6], (DEPTH, DEC_BATCH, CONV_WIDTH - 1, 2 * D_FF), 1.0),
        'page_table': page_table.reshape(DEC_BATCH, n_pages).astype(jnp.int32),
        'w_ada': nrm(ks[8], (DEPTH, D_MODEL, N_MOD * D_MODEL), 0.5 * D_MODEL ** -0.5),
        'b_ada': nrm(ks[9], (DEPTH, N_MOD * D_MODEL), 0.02),
        'g_pre_attn': gain(ks[10], (DEPTH, D_MODEL)),
        'g_post_attn': gain(ks[11], (DEPTH, D_MODEL)),
        'w_in': nrm(ks[12], (DEPTH, D_MODEL, 3 * MIX_WIDTH), D_MODEL ** -0.5),
        'w_out': nrm(ks[13], (DEPTH, MIX_WIDTH, D_MODEL), MIX_WIDTH ** -0.5),
        'lambda_q1': nrm(ks[14], (DEPTH, DIFF_QK_DIM), 0.1),
        'lambda_k1': nrm(ks[15], (DEPTH, DIFF_QK_DIM), 0.1),
        'lambda_q2': nrm(ks[16], (DEPTH, DIFF_QK_DIM), 0.1),
        'lambda_k2': nrm(ks[17], (DEPTH, DIFF_QK_DIM), 0.1),
        'g_subln': gain(ks[18], (DEPTH, HEAD_DIM)),
        'g_pre_mlp': gain(ks[19], (DEPTH, D_MODEL)),
        'g_post_mlp': gain(ks[20], (DEPTH, D_MODEL)),
        'w_up': nrm(ks[21], (DEPTH, D_MODEL, 2 * D_FF), D_MODEL ** -0.5),
        'conv_w': nrm(ks[22], (DEPTH, CONV_WIDTH, 2 * D_FF), CONV_WIDTH ** -0.5),
        'conv_b': nrm(ks[23], (DEPTH, 2 * D_FF), 0.02),
        'w_down': nrm(jax.random.fold_in(key, 99), (DEPTH, D_FF, D_MODEL), D_FF ** -0.5),
    }


def reference(x_prompt, x_sample, c_prompt, c_sample, cache_k, cache_v, state_conv, page_table,
              w_ada, b_ada, g_pre_attn, g_post_attn, w_in, w_out,
              lambda_q1, lambda_k1, lambda_q2, lambda_k2, g_subln,
              g_pre_mlp, g_post_mlp, w_up, conv_w, conv_b, w_down):
    xp, xs = x_prompt, x_sample
    kp_l, vp_l, cp_l, ks_l, vs_l, cs_l = [], [], [], [], [], []
    for l in range(DEPTH):
        lambda_init = 0.8 - 0.6 * math.exp(-0.3 * l)
        lam = (jnp.exp(jnp.sum(lambda_q1[l].astype(jnp.float32) * lambda_k1[l].astype(jnp.float32)))
               - jnp.exp(jnp.sum(lambda_q2[l].astype(jnp.float32) * lambda_k2[l].astype(jnp.float32)))
               + lambda_init)
        gs = g_subln[l]

        def attend_prompt(q, k, v):
            B, S = q.shape[0], q.shape[1]
            n_blk = S // Q_BLOCK
            pos = jnp.arange(S, dtype=jnp.int32)
            qb = q.reshape(B, n_blk, Q_BLOCK, N_HEADS, HEAD_DIM).transpose(1, 0, 2, 3, 4)
            pb = pos.reshape(n_blk, Q_BLOCK)
            out = lax.map(lambda a: _mixers(a[0], k, v, a[1], pos, lam, lambda_init, gs), (qb, pb))
            return out.transpose(1, 0, 2, 3).reshape(B, S, MIX_WIDTH)

        def attend_sample(q, k, v):
            B, T = q.shape[0], q.shape[1]
            past = page_table.shape[1] * PAGE_SIZE
            k_past = cache_k[l][page_table].reshape(B, past, N_HEADS, HEAD_DIM).astype(k.dtype)
            v_past = cache_v[l][page_table].reshape(B, past, N_HEADS, HEAD_DIM).astype(v.dtype)
            k_all = jnp.concatenate([k_past, k], axis=1)
            v_all = jnp.concatenate([v_past, v], axis=1)
            k_pos = jnp.arange(past + T, dtype=jnp.int32)
            q_pos = past + jnp.arange(T, dtype=jnp.int32)
            return _mixers(q, k_all, v_all, q_pos, k_pos, lam, lambda_init, gs)

        weights = (w_ada[l], b_ada[l], g_pre_attn[l], g_post_attn[l], w_in[l], w_out[l],
                   g_pre_mlp[l], g_post_mlp[l], w_up[l], conv_w[l], conv_b[l], w_down[l])
        conv0 = jnp.zeros((xp.shape[0], CONV_WIDTH - 1, 2 * D_FF), xp.dtype)
        xp, kp, vp, cp = _layer(xp, c_prompt, conv0, attend_prompt, *weights)
        xs, kk, vv, cs = _layer(xs, c_sample, state_conv[l].astype(xs.dtype), attend_sample, *weights)
        kp_l.append(kp); vp_l.append(vp); cp_l.append(cp)
        ks_l.append(kk); vs_l.append(vv); cs_l.append(cs)
    return (xp, xs, jnp.stack(kp_l), jnp.stack(vp_l), jnp.stack(cp_l),
            jnp.stack(ks_l), jnp.stack(vs_l), jnp.stack(cs_l))
```

```python
import functools
import math

import jax
import jax.numpy as jnp
import numpy as np
from jax import lax
from jax.experimental import pallas as pl
from jax.experimental.pallas import tpu as pltpu

F32 = jnp.float32
BF16 = jnp.bfloat16

D_MODEL = 1024
HEAD_DIM = 64
N_SB_HEADS = 8
N_DIFF_HEADS = 8
N_HEADS = N_SB_HEADS + N_DIFF_HEADS
SB_WIDTH = N_SB_HEADS * HEAD_DIM
DIFF_QK_DIM = HEAD_DIM // 2
D_FF = 2816
CONV_WIDTH = 3
PAGE_SIZE = 128
NORM_EPS = 1e-6
SUBLN_EPS = 1e-5
N_MOD = 6
LAMBDA_INIT = 0.8 - 0.6 * math.exp(-0.3 * 0)

LANES = 128
NEG_BIG = -1e30

VMEM_LIMIT = 56 * 1024 * 1024


def _dot(a, b):
    return jnp.dot(a, b, preferred_element_type=F32)


def _dot_nt(a, b):
    return lax.dot_general(a, b, (((1,), (1,)), ((), ())), preferred_element_type=F32)


def _rms(x, g, eps):
    return x * lax.rsqrt(jnp.mean(x * x, axis=-1, keepdims=True) + eps) * g


def _softplus(z):
    return jnp.maximum(z, 0.0) + jnp.log(1.0 + jnp.exp(-jnp.abs(z)))


def _split_bf16(x):
    hi = x.astype(BF16)
    lo = (x - hi.astype(F32)).astype(BF16)
    return hi, lo


def _mod_kernel(c_ref, w_ref, b_ref, o_ref):
    c = c_ref[...]
    s = c * (1.0 / (1.0 + jnp.exp(-c)))
    o_ref[...] = _dot(s.astype(BF16), w_ref[...].astype(BF16)) + b_ref[...]


def _modulation(c_all, w_ada, b_ada):
    n_rows = c_all.shape[0]
    n_out = w_ada.shape[1]
    tn = 1536
    return pl.pallas_call(
        _mod_kernel,
        out_shape=jax.ShapeDtypeStruct((n_rows, n_out), F32),
        grid=(n_out // tn,),
        in_specs=[
            pl.BlockSpec((n_rows, D_MODEL), lambda j: (0, 0)),
            pl.BlockSpec((D_MODEL, tn), lambda j: (0, j)),
            pl.BlockSpec((1, tn), lambda j: (0, j)),
        ],
        out_specs=pl.BlockSpec((n_rows, tn), lambda j: (0, j)),
        compiler_params=pltpu.CompilerParams(
            dimension_semantics=("arbitrary",), vmem_limit_bytes=VMEM_LIMIT),
        name="adaln_mod",
    )(c_all, w_ada, b_ada.reshape(1, n_out))


def _qkv_kernel(x_ref, sc_ref, sh_ref, g_ref, w_ref, q_ref, k_ref, v_ref, kb_ref, vb_ref,
                *, transposed):
    x = x_ref[...]
    h = _rms(x, g_ref[...], NORM_EPS) * (1.0 + sc_ref[...]) + sh_ref[...]
    qkv = _dot(h.astype(BF16), w_ref[...])
    lane = lax.broadcasted_iota(jnp.int32, (1, D_MODEL), 1)
    qscale = jnp.where(lane < SB_WIDTH, HEAD_DIM ** -0.5, DIFF_QK_DIM ** -0.5).astype(F32)
    q_ref[...] = (qkv[:, :D_MODEL] * qscale).astype(BF16)
    k = qkv[:, D_MODEL:2 * D_MODEL]
    v = qkv[:, 2 * D_MODEL:]
    k_ref[...] = k.T if transposed else k
    v_ref[...] = v.T if transposed else v
    kb_ref[...] = k.astype(BF16)
    vb_ref[...] = v.astype(BF16)


def _mod_spec(mod_rows, tm, tiles_per_group):
    r = mod_rows
    return pl.BlockSpec((None, r, D_MODEL), lambda i: (i // tiles_per_group, 0, 0))


def _qkv_proj(x2d, sc, sh, g, w_bf16, tm, tiles_per_group, transposed):
    n = x2d.shape[0]
    row_spec = pl.BlockSpec((tm, D_MODEL), lambda i: (i, 0))
    const2 = lambda i: (0, 0)
    out_bf16 = jax.ShapeDtypeStruct((n, D_MODEL), BF16)
    if transposed:
        t_len = tiles_per_group * tm
        out_f32 = jax.ShapeDtypeStruct((n // t_len, D_MODEL, t_len), F32)
        kv_spec = pl.BlockSpec((None, D_MODEL, tm),
                               lambda i: (i // tiles_per_group, 0, i % tiles_per_group))
    else:
        out_f32 = jax.ShapeDtypeStruct((n, D_MODEL), F32)
        kv_spec = row_spec
    return pl.pallas_call(
        functools.partial(_qkv_kernel, transposed=transposed),
        out_shape=(out_bf16, out_f32, out_f32, out_bf16, out_bf16),
        grid=(n // tm,),
        in_specs=[
            row_spec,
            _mod_spec(sc.shape[1], tm, tiles_per_group),
            _mod_spec(sh.shape[1], tm, tiles_per_group),
            pl.BlockSpec((1, D_MODEL), const2),
            pl.BlockSpec((D_MODEL, 3 * D_MODEL), const2),
        ],
        out_specs=(row_spec, kv_spec, kv_spec, row_spec, row_spec),
        compiler_params=pltpu.CompilerParams(
            dimension_semantics=("arbitrary",), vmem_limit_bytes=VMEM_LIMIT),
        name="qkv_proj",
    )(x2d, sc, sh, g.reshape(1, D_MODEL), w_bf16)


def _sb_block(z, pv, tri, run, acc, mask):
    sp = _softplus(z)
    log_not = -sp
    if mask is not None:
        log_not = jnp.where(mask, log_not, 0.0)
    hi, lo = _split_bf16(log_not)
    between = _dot(hi, tri) + _dot(lo, tri)
    w = jnp.exp(z - sp + between + run)
    if mask is not None:
        w = jnp.where(mask, w, 0.0)
    acc = acc + pv(w.astype(BF16))
    run = run + jnp.sum(log_not, axis=1, keepdims=True)
    return run, acc


def _softmax_block(s, pv, shift, m, l, acc):
    m_new = jnp.maximum(m, jnp.max(s, axis=1, keepdims=True) - shift)
    alpha = jnp.exp(m - m_new)
    p = jnp.exp(s - (m_new + shift))
    l = alpha * l + jnp.sum(p, axis=1, keepdims=True)
    acc = alpha * acc + pv(p.astype(BF16))
    return m_new, l, acc


def _lambda_value(lam_ref):
    lq1 = lam_ref[0:1, :]
    lk1 = lam_ref[1:2, :]
    lq2 = lam_ref[2:3, :]
    lk2 = lam_ref[3:4, :]
    return (jnp.exp(jnp.sum(lq1 * lk1, axis=1, keepdims=True))
            - jnp.exp(jnp.sum(lq2 * lk2, axis=1, keepdims=True)) + LAMBDA_INIT)


def _group_rms(o, g, lane, n_groups):
    sq = o * o
    ms = jnp.zeros_like(o)
    for h in range(n_groups):
        in_h = (lane >= HEAD_DIM * h) & (lane < HEAD_DIM * (h + 1))
        s_h = jnp.sum(jnp.where(in_h, sq, 0.0), axis=1, keepdims=True)
        ms = jnp.where(in_h, s_h * (1.0 / HEAD_DIM), ms)
    return o * lax.rsqrt(ms + SUBLN_EPS) * g


TQ = 256


def _sb_prompt_kernel(q_ref, k_ref, v_ref, tri_ref, o_ref):
    i = pl.program_id(2)
    qf = q_ref[...].astype(F32)
    lane = lax.broadcasted_iota(jnp.int32, (TQ, LANES), 1)
    row = lax.broadcasted_iota(jnp.int32, (TQ, TQ), 0)
    col = lax.broadcasted_iota(jnp.int32, (TQ, TQ), 1)
    strictly_causal = col < row
    tri = tri_ref[...]
    out = jnp.zeros((TQ, LANES), F32)
    for h in range(2):
        in_head = (lane >= HEAD_DIM * h) & (lane < HEAD_DIM * (h + 1))
        qm = jnp.where(in_head, qf, 0.0).astype(BF16)

        def step(j, run, acc, mask):
            start = pl.multiple_of(j * TQ, TQ)
            kb = k_ref[pl.ds(start, TQ), :]
            vb = v_ref[pl.ds(start, TQ), :]
            return _sb_block(_dot_nt(qm, kb), lambda w: _dot(w, vb), tri, run, acc, mask)

        run = jnp.zeros((TQ, 1), F32)
        acc = jnp.zeros((TQ, LANES), F32)
        run, acc = step(i, run, acc, strictly_causal)
        run, acc = lax.fori_loop(
            0, i, lambda jj, c: step(i - 1 - jj, c[0], c[1], None), (run, acc))
        out = jnp.where(in_head, acc, out)
    o_ref[...] = out.astype(BF16)


def _sb_prompt_attention(q, kb, vb, tri):
    b, s, _ = q.shape
    n_pairs = SB_WIDTH // LANES
    return pl.pallas_call(
        _sb_prompt_kernel,
        out_shape=jax.ShapeDtypeStruct((b, s, SB_WIDTH), BF16),
        grid=(b, n_pairs, s // TQ),
        in_specs=[
            pl.BlockSpec((None, TQ, LANES), lambda bi, p, i: (bi, i, p)),
            pl.BlockSpec((None, s, LANES), lambda bi, p, i: (bi, 0, p)),
            pl.BlockSpec((None, s, LANES), lambda bi, p, i: (bi, 0, p)),
            pl.BlockSpec((TQ, TQ), lambda bi, p, i: (0, 0)),
        ],
        out_specs=pl.BlockSpec((None, TQ, LANES), lambda bi, p, i: (bi, i, p)),
        compiler_params=pltpu.CompilerParams(
            dimension_semantics=("arbitrary", "arbitrary", "arbitrary"),
            vmem_limit_bytes=VMEM_LIMIT),
        name="sb_prompt_attn",
    )(q, kb, vb, tri)


def _diff_prompt_kernel(q_ref, k_ref, v_ref, slope_ref, lam_ref, g_ref, o_ref):
    i = pl.program_id(2)
    qf = q_ref[...].astype(F32)
    lane = lax.broadcasted_iota(jnp.int32, (TQ, LANES), 1)
    row = lax.broadcasted_iota(jnp.int32, (TQ, TQ), 0)
    col = lax.broadcasted_iota(jnp.int32, (TQ, TQ), 1)
    causal = col <= row
    rel = (col - row).astype(F32)
    lam = _lambda_value(lam_ref)
    out = jnp.zeros((TQ, LANES), F32)
    for h in range(2):
        in_head = (lane >= HEAD_DIM * h) & (lane < HEAD_DIM * (h + 1))
        slope = slope_ref[:, HEAD_DIM * h:HEAD_DIM * h + 1]
        slope_rel = slope * rel
        o_maps = []
        for c in range(2):
            lo_lane = HEAD_DIM * h + DIFF_QK_DIM * c
            in_map = (lane >= lo_lane) & (lane < lo_lane + DIFF_QK_DIM)
            qm = jnp.where(in_map, qf, 0.0).astype(BF16)

            def step(j, m, l, acc, masked):
                start = pl.multiple_of(j * TQ, TQ)
                kb = k_ref[pl.ds(start, TQ), :]
                vb = v_ref[pl.ds(start, TQ), :]
                s = _dot_nt(qm, kb) + slope_rel
                if masked:
                    s = jnp.where(causal, s, NEG_BIG)
                shift = slope * ((i - j) * TQ).astype(F32)
                return _softmax_block(s, lambda p: _dot(p, vb), shift, m, l, acc)

            m = jnp.full((TQ, 1), NEG_BIG, F32)
            l = jnp.zeros((TQ, 1), F32)
            acc = jnp.zeros((TQ, LANES), F32)
            m, l, acc = lax.fori_loop(
                0, i, lambda j, carry: step(j, *carry, False), (m, l, acc))
            m, l, acc = step(i, m, l, acc, True)
            o_maps.append(acc / l)
        out = jnp.where(in_head, o_maps[0] - lam * o_maps[1], out)
    y = _group_rms(out, g_ref[...], lane, 2) * (1.0 - LAMBDA_INIT)
    o_ref[...] = y.astype(BF16)


def _diff_prompt_attention(q, kb, vb, slopes, lam_vecs, g2):
    b, s, _ = q.shape
    n_pairs = (N_DIFF_HEADS * HEAD_DIM) // LANES
    first = SB_WIDTH // LANES
    const2 = lambda bi, p, i: (0, 0)
    return pl.pallas_call(
        _diff_prompt_kernel,
        out_shape=jax.ShapeDtypeStruct((b, s, N_DIFF_HEADS * HEAD_DIM), BF16),
        grid=(b, n_pairs, s // TQ),
        in_specs=[
            pl.BlockSpec((None, TQ, LANES), lambda bi, p, i: (bi, i, first + p)),
            pl.BlockSpec((None, s, LANES), lambda bi, p, i: (bi, 0, first + p)),
            pl.BlockSpec((None, s, LANES), lambda bi, p, i: (bi, 0, first + p)),
            pl.BlockSpec((None, 1, LANES), lambda bi, p, i: (p, 0, 0)),
            pl.BlockSpec((4, DIFF_QK_DIM), const2),
            pl.BlockSpec((1, LANES), const2),
        ],
        out_specs=pl.BlockSpec((None, TQ, LANES), lambda bi, p, i: (bi, i, p)),
        compiler_params=pltpu.CompilerParams(
            dimension_semantics=("arbitrary", "arbitrary", "arbitrary"),
            vmem_limit_bytes=VMEM_LIMIT),
        name="diff_prompt_attn",
    )(q, kb, vb, slopes, lam_vecs, g2)


PAGES_PER_STEP = 4
DEC_T = 8
SB_ROWS = N_SB_HEADS * DEC_T
DF_ROWS = N_DIFF_HEADS * 2 * DEC_T


def _dec_attn_kernel(pt_ref, q_ref, kn_ref, vn_ref, *rest, past_len):
    del pt_ref
    kp_refs = rest[:PAGES_PER_STEP]
    vp_refs = rest[PAGES_PER_STEP:2 * PAGES_PER_STEP]
    (tri_ref, slope_ref, lam_ref, g_ref, o_ref,
     wsb_ref, wdf_ref, run_ref, accsb_ref, m_ref, l_ref, accdf_ref) = rest[2 * PAGES_PER_STEP:]
    step_id = pl.program_id(1)
    n_steps = pl.num_programs(1)
    df_width = N_DIFF_HEADS * HEAD_DIM

    key_lane = lax.broadcasted_iota(jnp.int32, (DF_ROWS, PAGE_SIZE), 1)
    t_df = lax.broadcasted_iota(jnp.int32, (DF_ROWS, 1), 0) & (DEC_T - 1)
    slope = slope_ref[...]
    slope_key = slope * key_lane.astype(F32)
    tri = tri_ref[...]

    def sb_update(z, pv, mask):
        run, acc = _sb_block(z, pv, tri, run_ref[...], accsb_ref[...], mask)
        run_ref[...] = run
        accsb_ref[...] = acc

    def df_update(s, pv, base, mask):
        s = s + slope_key
        if mask is not None:
            s = jnp.where(mask, s, NEG_BIG)
        shift = slope * (past_len + t_df - base).astype(F32)
        m, l, acc = _softmax_block(s, pv, shift, m_ref[...], l_ref[...], accdf_ref[...])
        m_ref[...] = m
        l_ref[...] = l
        accdf_ref[...] = acc

    @pl.when(step_id == 0)
    def _init():
        qf = q_ref[...].astype(F32)
        q_sb = jnp.concatenate([qf[:, :SB_WIDTH]] * N_SB_HEADS, axis=0)
        r = lax.broadcasted_iota(jnp.int32, (SB_ROWS, SB_WIDTH), 0)
        c = lax.broadcasted_iota(jnp.int32, (SB_ROWS, SB_WIDTH), 1)
        wsb_ref[...] = jnp.where((r >> 3) == (c >> 6), q_sb, 0.0).astype(BF16)
        q_df = jnp.concatenate([qf[:, SB_WIDTH:]] * (2 * N_DIFF_HEADS), axis=0)
        r = lax.broadcasted_iota(jnp.int32, (DF_ROWS, df_width), 0)
        c = lax.broadcasted_iota(jnp.int32, (DF_ROWS, df_width), 1)
        wdf_ref[...] = jnp.where((r >> 3) == (c >> 5), q_df, 0.0).astype(BF16)
        run_ref[...] = jnp.zeros_like(run_ref)
        accsb_ref[...] = jnp.zeros_like(accsb_ref)
        m_ref[...] = jnp.full_like(m_ref, NEG_BIG)
        l_ref[...] = jnp.zeros_like(l_ref)
        accdf_ref[...] = jnp.zeros_like(accdf_ref)
        pad = jnp.zeros((PAGE_SIZE - DEC_T, D_MODEL), F32)
        kn = jnp.concatenate([kn_ref[...], pad], axis=0).astype(BF16)
        vn = jnp.concatenate([vn_ref[...], pad], axis=0).astype(BF16)
        key_sb = lax.broadcasted_iota(jnp.int32, (SB_ROWS, PAGE_SIZE), 1)
        t_sb = lax.broadcasted_iota(jnp.int32, (SB_ROWS, PAGE_SIZE), 0) & (DEC_T - 1)
        sb_update(_dot_nt(wsb_ref[...], kn[:, :SB_WIDTH]),
                  lambda w: _dot(w, vn[:, :SB_WIDTH]), key_sb < t_sb)
        df_update(_dot_nt(wdf_ref[...], kn[:, SB_WIDTH:]),
                  lambda p: _dot(p, vn[:, SB_WIDTH:]), past_len, key_lane <= t_df)

    n_pages = n_steps * PAGES_PER_STEP
    for r in range(PAGES_PER_STEP):
        page = n_pages - 1 - (step_id * PAGES_PER_STEP + r)
        kp = kp_refs[r][...].astype(BF16)
        vp = vp_refs[r][...].astype(BF16)
        sb_update(_dot(wsb_ref[...], kp[:SB_WIDTH, :]),
                  lambda w: _dot_nt(w, vp[:SB_WIDTH, :]), None)
        df_update(_dot(wdf_ref[...], kp[SB_WIDTH:, :]),
                  lambda p: _dot_nt(p, vp[SB_WIDTH:, :]), page * PAGE_SIZE, None)

    @pl.when(step_id == n_steps - 1)
    def _finish():
        lane = lax.broadcasted_iota(jnp.int32, (DEC_T, SB_WIDTH), 1)
        lam = _lambda_value(lam_ref)
        acc_sb = accsb_ref[...]
        acc_df = accdf_ref[...] / l_ref[...]
        o_sb = jnp.zeros((DEC_T, SB_WIDTH), F32)
        o_df = jnp.zeros((DEC_T, df_width), F32)
        for h in range(N_SB_HEADS):
            in_h = (lane >> 6) == h
            o_sb = jnp.where(in_h, acc_sb[DEC_T * h:DEC_T * (h + 1), :], o_sb)
            r0 = 2 * DEC_T * h
            o_h = acc_df[r0:r0 + DEC_T, :] - lam * acc_df[r0 + DEC_T:r0 + 2 * DEC_T, :]
            o_df = jnp.where(in_h, o_h, o_df)
        y_df = _group_rms(o_df, g_ref[...], lane, N_DIFF_HEADS) * (1.0 - LAMBDA_INIT)
        o_ref[:, :SB_WIDTH] = o_sb.astype(BF16)
        o_ref[:, SB_WIDTH:] = y_df.astype(BF16)


def _dec_attention(page_table, q, k_new, v_new, cache_k, cache_v, tri, slope_rows, lam_vecs, g8):
    n_seq, n_pages = page_table.shape
    past_len = n_pages * PAGE_SIZE
    n_steps = n_pages // PAGES_PER_STEP
    df_width = N_DIFF_HEADS * HEAD_DIM

    def page_spec(r):
        def idx(b, s, pt):
            return (pt[b, n_pages - 1 - (s * PAGES_PER_STEP + r)], 0, 0)
        return pl.BlockSpec((None, D_MODEL, PAGE_SIZE), idx)

    seq_spec = pl.BlockSpec((None, DEC_T, D_MODEL), lambda b, s, pt: (b, 0, 0))
    const2 = lambda b, s, pt: (0, 0)
    grid_spec = pltpu.PrefetchScalarGridSpec(
        num_scalar_prefetch=1,
        grid=(n_seq, n_steps),
        in_specs=[seq_spec, seq_spec, seq_spec]
        + [page_spec(r) for r in range(PAGES_PER_STEP)]
        + [page_spec(r) for r in range(PAGES_PER_STEP)]
        + [
            pl.BlockSpec((PAGE_SIZE, PAGE_SIZE), const2),
            pl.BlockSpec((DF_ROWS, 1), const2),
            pl.BlockSpec((4, DIFF_QK_DIM), const2),
            pl.BlockSpec((1, df_width), const2),
        ],
        out_specs=seq_spec,
        scratch_shapes=[
            pltpu.VMEM((SB_ROWS, SB_WIDTH), BF16),
            pltpu.VMEM((DF_ROWS, df_width), BF16),
            pltpu.VMEM((SB_ROWS, 1), F32),
            pltpu.VMEM((SB_ROWS, SB_WIDTH), F32),
            pltpu.VMEM((DF_ROWS, 1), F32),
            pltpu.VMEM((DF_ROWS, 1), F32),
            pltpu.VMEM((DF_ROWS, df_width), F32),
        ],
    )
    return pl.pallas_call(
        functools.partial(_dec_attn_kernel, past_len=past_len),
        out_shape=jax.ShapeDtypeStruct((n_seq, DEC_T, D_MODEL), BF16),
        grid_spec=grid_spec,
        compiler_params=pltpu.CompilerParams(
            dimension_semantics=("arbitrary", "arbitrary"), vmem_limit_bytes=VMEM_LIMIT),
        name="dec_attn",
    )(page_table, q, k_new, v_new, *([cache_k] * PAGES_PER_STEP), *([cache_v] * PAGES_PER_STEP),
      tri, slope_rows, lam_vecs, g8)


def _attn_out_kernel(ma_ref, mb_ref, x_ref, wa_ref, wb_ref, gpost_ref, gt_ref,
                     gpre_ref, sc_ref, sh_ref, x1_ref, h_ref):
    y = _dot(ma_ref[...], wa_ref[...]) + _dot(mb_ref[...], wb_ref[...])
    x1 = x_ref[...] + gt_ref[...] * _rms(y, gpost_ref[...], NORM_EPS)
    x1_ref[...] = x1
    h = _rms(x1, gpre_ref[...], NORM_EPS) * (1.0 + sc_ref[...]) + sh_ref[...]
    h_ref[...] = h.astype(BF16)


def _attn_out(mixed_a, a_col, mixed_b, b_col, x2d, w_out_bf16, g_post, gt, g_pre, sc, sh,
              tm, tiles_per_group):
    n = x2d.shape[0]
    half = D_MODEL // 2
    row_spec = pl.BlockSpec((tm, D_MODEL), lambda i: (i, 0))
    const2 = lambda i: (0, 0)
    vec_spec = pl.BlockSpec((1, D_MODEL), const2)
    mod = lambda a: _mod_spec(a.shape[1], tm, tiles_per_group)
    return pl.pallas_call(
        _attn_out_kernel,
        out_shape=(jax.ShapeDtypeStruct((n, D_MODEL), F32),
                   jax.ShapeDtypeStruct((n, D_MODEL), BF16)),
        grid=(n // tm,),
        in_specs=[
            pl.BlockSpec((tm, half), lambda i: (i, a_col)),
            pl.BlockSpec((tm, half), lambda i: (i, b_col)),
            row_spec,
            pl.BlockSpec((half, D_MODEL), lambda i: (0, 0)),
            pl.BlockSpec((half, D_MODEL), lambda i: (1, 0)),
            vec_spec, mod(gt), vec_spec, mod(sc), mod(sh),
        ],
        out_specs=(row_spec, row_spec),
        compiler_params=pltpu.CompilerParams(
            dimension_semantics=("arbitrary",), vmem_limit_bytes=VMEM_LIMIT),
        name="attn_out",
    )(mixed_a, mixed_b, x2d, w_out_bf16, w_out_bf16, g_post.reshape(1, D_MODEL), gt,
      g_pre.reshape(1, D_MODEL), sc, sh)


UP_CHUNK = 1408


def _up_conv_kernel(h_ref, prev_ref, wup_ref, cw_ref, cb_ref, g_ref, st_ref, carry_ref, *, tm):
    t = pl.program_id(1)

    @pl.when(t == 0)
    def _load_state():
        carry_ref[...] = prev_ref[...]

    h = h_ref[...]
    row = lax.broadcasted_iota(jnp.int32, (tm, 1), 0)

    def conv_cols(c0):
        u = _dot(h, wup_ref[:, c0:c0 + UP_CHUNK])
        p0 = carry_ref[0:1, c0:c0 + UP_CHUNK]
        p1 = carry_ref[1:2, c0:c0 + UP_CHUNK]
        u1 = jnp.where(row == 0, p1, pltpu.roll(u, 1, 0))
        u2 = jnp.where(row == 0, p0, jnp.where(row == 1, p1, pltpu.roll(u, 2, 0)))
        w0 = cw_ref[0:1, c0:c0 + UP_CHUNK]
        w1 = cw_ref[1:2, c0:c0 + UP_CHUNK]
        w2 = cw_ref[2:3, c0:c0 + UP_CHUNK]
        conv = cb_ref[:, c0:c0 + UP_CHUNK] + w0 * u2 + w1 * u1 + w2 * u
        last = u[tm - 2:tm, :]
        carry_ref[:, c0:c0 + UP_CHUNK] = last
        st_ref[:, c0:c0 + UP_CHUNK] = last
        return conv

    for ch in range(D_FF // UP_CHUNK):
        a = conv_cols(ch * UP_CHUNK)
        b = conv_cols(D_FF + ch * UP_CHUNK)
        gate = a * (1.0 / (1.0 + jnp.exp(-a))) * b
        g_ref[:, ch * UP_CHUNK:(ch + 1) * UP_CHUNK] = gate.astype(BF16)


def _up_conv(h3d, conv_prev, w_up_bf16, conv_w, conv_b, tm):
    nb, t_len, _ = h3d.shape
    const2 = lambda b, t: (0, 0)
    return pl.pallas_call(
        functools.partial(_up_conv_kernel, tm=tm),
        out_shape=(jax.ShapeDtypeStruct((nb, t_len, D_FF), BF16),
                   jax.ShapeDtypeStruct((nb, CONV_WIDTH - 1, 2 * D_FF), F32)),
        grid=(nb, t_len // tm),
        in_specs=[
            pl.BlockSpec((None, tm, D_MODEL), lambda b, t: (b, t, 0)),
            pl.BlockSpec((None, CONV_WIDTH - 1, 2 * D_FF), lambda b, t: (b, 0, 0)),
            pl.BlockSpec((D_MODEL, 2 * D_FF), const2),
            pl.BlockSpec((CONV_WIDTH, 2 * D_FF), const2),
            pl.BlockSpec((1, 2 * D_FF), const2),
        ],
        out_specs=(pl.BlockSpec((None, tm, D_FF), lambda b, t: (b, t, 0)),
                   pl.BlockSpec((None, CONV_WIDTH - 1, 2 * D_FF), lambda b, t: (b, 0, 0))),
        scratch_shapes=[pltpu.VMEM((CONV_WIDTH - 1, 2 * D_FF), F32)],
        compiler_params=pltpu.CompilerParams(
            dimension_semantics=("arbitrary", "arbitrary"), vmem_limit_bytes=VMEM_LIMIT),
        name="up_conv_gate",
    )(h3d, conv_prev, w_up_bf16, conv_w, conv_b.reshape(1, 2 * D_FF))


def _down_kernel(g_ref, x1_ref, wd_ref, gpost_ref, gt_ref, y_ref):
    f = _dot(g_ref[...], wd_ref[...])
    y_ref[...] = x1_ref[...] + gt_ref[...] * _rms(f, gpost_ref[...], NORM_EPS)


def _down_proj(g2d, x1, w_down_bf16, g_post, gt, tm, tiles_per_group):
    n = x1.shape[0]
    row_spec = pl.BlockSpec((tm, D_MODEL), lambda i: (i, 0))
    const2 = lambda i: (0, 0)
    return pl.pallas_call(
        _down_kernel,
        out_shape=jax.ShapeDtypeStruct((n, D_MODEL), F32),
        grid=(n // tm,),
        in_specs=[
            pl.BlockSpec((tm, D_FF), lambda i: (i, 0)),
            row_spec,
            pl.BlockSpec((D_FF, D_MODEL), const2),
            pl.BlockSpec((1, D_MODEL), const2),
            _mod_spec(gt.shape[1], tm, tiles_per_group),
        ],
        out_specs=row_spec,
        compiler_params=pltpu.CompilerParams(
            dimension_semantics=("arbitrary",), vmem_limit_bytes=VMEM_LIMIT),
        name="down_proj",
    )(g2d, x1, w_down_bf16, g_post.reshape(1, D_MODEL), gt)


def _suffix_operator(n):
    idx = np.arange(n)
    return jnp.asarray((idx[:, None] > idx[None, :]).astype(np.float32), dtype=BF16)


def _alibi_slopes_np():
    return (2.0 ** (-(8.0 / N_DIFF_HEADS) * np.arange(1, N_DIFF_HEADS + 1))).astype(np.float32)


def kernel(x_prompt, x_sample, c_prompt, c_sample, cache_k, cache_v, state_conv, page_table,
           w_ada, b_ada, g_pre_attn, g_post_attn, w_in, w_out, lambda_q1, lambda_k1,
           lambda_q2, lambda_k2, g_subln, g_pre_mlp, g_post_mlp, w_up, conv_w, conv_b, w_down):
    layer = 0
    n_b, seq, _ = x_prompt.shape
    n_dec, dec_t, _ = x_sample.shape
    assert dec_t == DEC_T
    n_pool = cache_k.shape[1]

    w_in_b = w_in[layer].astype(BF16)
    w_out_b = w_out[layer].astype(BF16)
    w_up_b = w_up[layer].astype(BF16)
    w_down_b = w_down[layer].astype(BF16)

    c_all = jnp.concatenate([c_prompt, c_sample], axis=0)
    mod = _modulation(c_all, w_ada[layer], b_ada[layer])
    mod_p = [m.reshape(n_b, 1, D_MODEL) for m in jnp.split(mod[:n_b], N_MOD, axis=-1)]
    mod_s = [jnp.repeat(m, DEC_T, axis=0).reshape(1, n_dec * DEC_T, D_MODEL)
             for m in jnp.split(mod[n_b:], N_MOD, axis=-1)]

    slopes = _alibi_slopes_np()
    slope_pairs = jnp.asarray(np.repeat(slopes, HEAD_DIM).reshape(N_DIFF_HEADS // 2, 1, LANES))
    slope_rows = jnp.asarray(np.repeat(slopes, 2 * DEC_T).reshape(DF_ROWS, 1))
    lam_vecs = jnp.stack([lambda_q1[layer], lambda_k1[layer],
                          lambda_q2[layer], lambda_k2[layer]]).astype(F32)
    g_sub = g_subln[layer].astype(F32)
    g2 = jnp.tile(g_sub, 2).reshape(1, LANES)
    g8 = jnp.tile(g_sub, N_DIFF_HEADS).reshape(1, N_DIFF_HEADS * HEAD_DIM)

    tm_p = 256
    tiles_p = seq // tm_p
    xp2d = x_prompt.reshape(n_b * seq, D_MODEL)
    sh_a, sc_a, gt_a, sh_m, sc_m, gt_m = mod_p
    q_p, kt_p, vt_p, kb_p, vb_p = _qkv_proj(xp2d, sc_a, sh_a, g_pre_attn[layer], w_in_b,
                                            tm_p, tiles_p, True)
    as3 = lambda a: a.reshape(n_b, seq, D_MODEL)
    mixed_sb = _sb_prompt_attention(as3(q_p), as3(kb_p), as3(vb_p), _suffix_operator(TQ))
    mixed_df = _diff_prompt_attention(as3(q_p), as3(kb_p), as3(vb_p), slope_pairs, lam_vecs, g2)
    x1_p, h_p = _attn_out(mixed_sb.reshape(n_b * seq, SB_WIDTH), 0,
                          mixed_df.reshape(n_b * seq, SB_WIDTH), 0,
                          xp2d, w_out_b, g_post_attn[layer], gt_a, g_pre_mlp[layer], sc_m, sh_m,
                          tm_p, tiles_p)
    conv0 = jnp.zeros((n_b, CONV_WIDTH - 1, 2 * D_FF), F32)
    gate_p, conv_p = _up_conv(h_p.reshape(n_b, seq, D_MODEL), conv0, w_up_b,
                              conv_w[layer], conv_b[layer], tm_p)
    y_p = _down_proj(gate_p.reshape(n_b * seq, D_FF), x1_p, w_down_b, g_post_mlp[layer],
                     gt_m, tm_p, tiles_p)

    tm_s = n_dec * DEC_T
    xs2d = x_sample.reshape(tm_s, D_MODEL)
    sh_a, sc_a, gt_a, sh_m, sc_m, gt_m = mod_s
    q_s, k_s, v_s, _, _ = _qkv_proj(xs2d, sc_a, sh_a, g_pre_attn[layer], w_in_b, tm_s, 1, False)
    per_seq = lambda a: a.reshape(n_dec, DEC_T, D_MODEL)
    pages = lambda c: c[layer].transpose(0, 2, 3, 1).reshape(n_pool, D_MODEL, PAGE_SIZE)
    mixed_s = _dec_attention(
        page_table, per_seq(q_s), per_seq(k_s), per_seq(v_s), pages(cache_k), pages(cache_v),
        _suffix_operator(PAGE_SIZE), slope_rows, lam_vecs, g8)
    mixed_s2d = mixed_s.reshape(tm_s, D_MODEL)
    x1_s, h_s = _attn_out(mixed_s2d, 0, mixed_s2d, 1, xs2d, w_out_b, g_post_attn[layer], gt_a,
                          g_pre_mlp[layer], sc_m, sh_m, tm_s, 1)
    gate_s, conv_s = _up_conv(h_s.reshape(n_dec, DEC_T, D_MODEL), state_conv[layer].astype(F32),
                              w_up_b, conv_w[layer], conv_b[layer], DEC_T)
    y_s = _down_proj(gate_s.reshape(tm_s, D_FF), x1_s, w_down_b, g_post_mlp[layer], gt_m, tm_s, 1)

    heads = lambda a, b, t: a.reshape(1, b, t, N_HEADS, HEAD_DIM)
    heads_t = lambda a: a.reshape(1, n_b, N_HEADS, HEAD_DIM, seq).transpose(0, 1, 4, 2, 3)
    return (y_p.reshape(n_b, seq, D_MODEL),
            y_s.reshape(n_dec, DEC_T, D_MODEL),
            heads_t(kt_p), heads_t(vt_p), conv_p[None],
            heads(k_s, n_dec, DEC_T), heads(v_s, n_dec, DEC_T), conv_s[None])
```

```python
import functools
import math

import jax
import jax.numpy as jnp
import ml_dtypes
import numpy as np
from jax import lax
from jax.experimental import pallas as pl
from jax.experimental.pallas import tpu as pltpu

F32 = jnp.float32
BF16 = jnp.bfloat16

D_MODEL = 1024
HEAD_DIM = 64
N_SB_HEADS = 8
N_DIFF_HEADS = 8
N_HEADS = N_SB_HEADS + N_DIFF_HEADS
SB_WIDTH = N_SB_HEADS * HEAD_DIM
DF_WIDTH = N_DIFF_HEADS * HEAD_DIM
DIFF_QK_DIM = HEAD_DIM // 2
D_FF = 2816
CONV_WIDTH = 3
PAGE_SIZE = 128
NORM_EPS = 1e-6
SUBLN_EPS = 1e-5
N_MOD = 6
LAMBDA_INIT = 0.8 - 0.6 * math.exp(-0.3 * 0)

LANES = 128
NEG_BIG = -1e30
LOG2E = math.log2(math.e)
SB_DEAD_LOG2 = 104.0 * LOG2E + 1.0

VMEM_LIMIT = 56 * 1024 * 1024


def _dot(a, b):
    return jnp.dot(a, b, preferred_element_type=F32)


def _dot_nt(a, b):
    return lax.dot_general(a, b, (((1,), (1,)), ((), ())), preferred_element_type=F32)


def _rms(x, g, eps):
    return x * lax.rsqrt(jnp.mean(x * x, axis=-1, keepdims=True) + eps) * g


def _softplus2(z):
    return jnp.maximum(z, 0.0) + jnp.log2(1.0 + jnp.exp2(-jnp.abs(z)))


def _split_bf16(x):
    hi = x.astype(BF16)
    lo = (x - hi.astype(F32)).astype(BF16)
    return hi, lo


def _mod_kernel(c_ref, w_ref, b_ref, o_ref):
    c = c_ref[...]
    s = c * (1.0 / (1.0 + jnp.exp(-c)))
    o_ref[...] = _dot(s.astype(BF16), w_ref[...].astype(BF16)) + b_ref[...]


def _modulation(c_all, w_ada, b_ada):
    n_rows = c_all.shape[0]
    n_out = w_ada.shape[1]
    tn = 1536
    return pl.pallas_call(
        _mod_kernel,
        out_shape=jax.ShapeDtypeStruct((n_rows, n_out), F32),
        grid=(n_out // tn,),
        in_specs=[
            pl.BlockSpec((n_rows, D_MODEL), lambda j: (0, 0)),
            pl.BlockSpec((D_MODEL, tn), lambda j: (0, j)),
            pl.BlockSpec((1, tn), lambda j: (0, j)),
        ],
        out_specs=pl.BlockSpec((n_rows, tn), lambda j: (0, j)),
        compiler_params=pltpu.CompilerParams(
            dimension_semantics=("arbitrary",), vmem_limit_bytes=VMEM_LIMIT),
        name="adaln_mod",
    )(c_all, w_ada, b_ada.reshape(1, n_out))


def _scaled_qkv(x_ref, sc_ref, sh_ref, g_ref, w_ref):
    x = x_ref[...]
    h = _rms(x, g_ref[...], NORM_EPS) * (1.0 + sc_ref[...]) + sh_ref[...]
    qkv = _dot(h.astype(BF16), w_ref[...])
    lane = lax.broadcasted_iota(jnp.int32, (1, D_MODEL), 1)
    qscale = jnp.where(lane < SB_WIDTH, LOG2E * HEAD_DIM ** -0.5,
                       LOG2E * DIFF_QK_DIM ** -0.5).astype(F32)
    return qkv[:, :D_MODEL] * qscale, qkv[:, D_MODEL:2 * D_MODEL], qkv[:, 2 * D_MODEL:]


def _qkv_rows_kernel(x_ref, sc_ref, sh_ref, g_ref, w_ref, q_ref, k_ref, v_ref):
    q, k, v = _scaled_qkv(x_ref, sc_ref, sh_ref, g_ref, w_ref)
    q_ref[...] = q.astype(BF16)
    k_ref[...] = k
    v_ref[...] = v


def _qkv_prompt_kernel(x_ref, sc_ref, sh_ref, g_ref, w_ref, qt_ref, kt_ref, vt_ref, kb_ref, vtb_ref):
    q, k, v = _scaled_qkv(x_ref, sc_ref, sh_ref, g_ref, w_ref)
    qt_ref[...] = q.T.astype(BF16)
    kt_ref[...] = k.T
    vt = v.T
    vt_ref[...] = vt
    vtb_ref[...] = vt.astype(BF16)
    kb_ref[...] = k.astype(BF16)


def _mod_spec(mod_rows, tiles_per_group):
    return pl.BlockSpec((None, mod_rows, D_MODEL), lambda i: (i // tiles_per_group, 0, 0))


def _qkv_in_specs(sc, sh, tm, tiles_per_group):
    const2 = lambda i: (0, 0)
    return [
        pl.BlockSpec((tm, D_MODEL), lambda i: (i, 0)),
        _mod_spec(sc.shape[1], tiles_per_group),
        _mod_spec(sh.shape[1], tiles_per_group),
        pl.BlockSpec((1, D_MODEL), const2),
        pl.BlockSpec((D_MODEL, 3 * D_MODEL), const2),
    ]


def _qkv_rows(x2d, sc, sh, g, w_bf16, tm):
    n = x2d.shape[0]
    row_spec = pl.BlockSpec((tm, D_MODEL), lambda i: (i, 0))
    out_f32 = jax.ShapeDtypeStruct((n, D_MODEL), F32)
    return pl.pallas_call(
        _qkv_rows_kernel,
        out_shape=(jax.ShapeDtypeStruct((n, D_MODEL), BF16), out_f32, out_f32),
        grid=(n // tm,),
        in_specs=_qkv_in_specs(sc, sh, tm, 1),
        out_specs=(row_spec, row_spec, row_spec),
        compiler_params=pltpu.CompilerParams(
            dimension_semantics=("arbitrary",), vmem_limit_bytes=VMEM_LIMIT),
        name="qkv_rows",
    )(x2d, sc, sh, g.reshape(1, D_MODEL), w_bf16)


def _qkv_prompt(x2d, sc, sh, g, w_bf16, n_b, seq, tm):
    tiles = seq // tm
    t_spec = pl.BlockSpec((None, D_MODEL, tm), lambda i: (i // tiles, 0, i % tiles))
    t_f32 = jax.ShapeDtypeStruct((n_b, D_MODEL, seq), F32)
    return pl.pallas_call(
        _qkv_prompt_kernel,
        out_shape=(jax.ShapeDtypeStruct((n_b, D_MODEL, seq), BF16), t_f32, t_f32,
                   jax.ShapeDtypeStruct((n_b * seq, D_MODEL), BF16),
                   jax.ShapeDtypeStruct((n_b, tiles, D_MODEL, tm), BF16)),
        grid=(n_b * tiles,),
        in_specs=_qkv_in_specs(sc, sh, tm, tiles),
        out_specs=(t_spec, t_spec, t_spec,
                   pl.BlockSpec((tm, D_MODEL), lambda i: (i, 0)),
                   pl.BlockSpec((None, None, D_MODEL, tm), lambda i: (i // tiles, i % tiles, 0, 0))),
        compiler_params=pltpu.CompilerParams(
            dimension_semantics=("arbitrary",), vmem_limit_bytes=VMEM_LIMIT),
        name="qkv_prompt",
    )(x2d, sc, sh, g.reshape(1, D_MODEL), w_bf16)


def _lambda_value(lam_ref):
    lq1 = lam_ref[0:1, :]
    lk1 = lam_ref[1:2, :]
    lq2 = lam_ref[2:3, :]
    lk2 = lam_ref[3:4, :]
    return (jnp.exp(jnp.sum(lq1 * lk1, axis=1, keepdims=True))
            - jnp.exp(jnp.sum(lq2 * lk2, axis=1, keepdims=True)) + LAMBDA_INIT)


TQ = 256


def _prompt_specs(s, first_pair):
    n_blk = s // TQ
    return [
        pl.BlockSpec((None, LANES, TQ), lambda bi, p, i: (bi, first_pair + p, i)),
        pl.BlockSpec((None, n_blk, TQ, LANES), lambda bi, p, i: (bi, 0, 0, first_pair + p)),
        pl.BlockSpec((None, n_blk, LANES, TQ), lambda bi, p, i: (bi, 0, first_pair + p, 0)),
    ]


def _sb_prompt_kernel(qt_ref, k_ref, vt_ref, usuf_ref, o_ref, acc_ref):
    i = pl.program_id(2)
    qt = qt_ref[...].astype(F32)
    drow = lax.broadcasted_iota(jnp.int32, (LANES, TQ), 0)
    key = lax.broadcasted_iota(jnp.int32, (TQ, TQ), 0)
    qry = lax.broadcasted_iota(jnp.int32, (TQ, TQ), 1)
    strictly_causal = key < qry
    usuf = usuf_ref[...]
    in_head = [(drow >= HEAD_DIM * h) & (drow < HEAD_DIM * (h + 1)) for h in range(2)]
    qtm = [jnp.where(in_head[h], qt, 0.0).astype(BF16) for h in range(2)]

    def step(j, runs, masked):
        kb = k_ref[j]
        vt = vt_ref[j]
        zs = [_dot(kb, qtm[h]) for h in range(2)]
        log_sig, parts, new_runs = [], [], []
        for h in range(2):
            sp = _softplus2(zs[h])
            log_sig.append(zs[h] - sp)
            if masked:
                sp = jnp.where(strictly_causal, sp, 0.0)
            parts.append(_split_bf16(sp))
            new_runs.append(runs[h] + jnp.sum(sp, axis=0, keepdims=True))
        between = [_dot(usuf, parts[h][0]) + _dot(usuf, parts[h][1]) for h in range(2)]
        weights = []
        for h in range(2):
            w = jnp.exp2(log_sig[h] - between[h] - runs[h])
            if masked:
                w = jnp.where(strictly_causal, w, 0.0)
            weights.append(w.astype(BF16))
        outs = [_dot(vt, weights[h]) for h in range(2)]
        for h in range(2):
            acc_ref[h] = acc_ref[h] + outs[h]
        return tuple(new_runs)

    def alive(runs):
        return (jnp.min(jnp.minimum(runs[0], runs[1])) < SB_DEAD_LOG2).astype(jnp.int32)

    acc_ref[...] = jnp.zeros_like(acc_ref)
    zero = jnp.zeros((1, TQ), F32)
    runs = step(i, (zero, zero), True)

    def cond(c):
        return (c[0] < i) & (c[1] > 0)

    def body(c):
        runs = step(i - 1 - c[0], (c[2], c[3]), False)
        return c[0] + 1, alive(runs), runs[0], runs[1]

    lax.while_loop(cond, body, (jnp.int32(0), alive(runs), runs[0], runs[1]))
    out_t = jnp.where(in_head[0], acc_ref[0], acc_ref[1])
    o_ref[...] = out_t.T.astype(BF16)


def _sb_prompt_attention(qt, kb4, vtb4, usuf):
    b, _, s = qt.shape
    n_pairs = SB_WIDTH // LANES
    return pl.pallas_call(
        _sb_prompt_kernel,
        out_shape=jax.ShapeDtypeStruct((b, s, SB_WIDTH), BF16),
        grid=(b, n_pairs, s // TQ),
        in_specs=_prompt_specs(s, 0) + [pl.BlockSpec((TQ, TQ), lambda bi, p, i: (0, 0))],
        out_specs=pl.BlockSpec((None, TQ, LANES), lambda bi, p, i: (bi, i, p)),
        scratch_shapes=[pltpu.VMEM((2, LANES, TQ), F32)],
        compiler_params=pltpu.CompilerParams(
            dimension_semantics=("arbitrary", "arbitrary", "arbitrary"),
            vmem_limit_bytes=VMEM_LIMIT),
        name="sb_prompt_attn",
    )(qt, kb4, vtb4, usuf)


def _diff_prompt_kernel(qt_ref, k_ref, vt_ref, slope_ref, sfeat_ref, kfeat_ref, lam_ref, g_ref,
                        o_ref, acc_ref):
    i = pl.program_id(2)
    qt = qt_ref[...].astype(F32)
    drow = lax.broadcasted_iota(jnp.int32, (LANES, TQ), 0)
    key = lax.broadcasted_iota(jnp.int32, (TQ, TQ), 0)
    qry = lax.broadcasted_iota(jnp.int32, (TQ, TQ), 1)
    causal = key <= qry
    lam = _lambda_value(lam_ref)
    slopes = [slope_ref[:, HEAD_DIM * h:HEAD_DIM * h + 1] for h in range(2)]
    kfeat = kfeat_ref[...]
    qext = []
    for h in range(2):
        for c in range(2):
            lo_row = HEAD_DIM * h + DIFF_QK_DIM * c
            in_map = (drow >= lo_row) & (drow < lo_row + DIFF_QK_DIM)
            qtm = jnp.where(in_map, qt, 0.0).astype(BF16)
            qext.append(jnp.concatenate([qtm, sfeat_ref[h]], axis=0))

    def step(j, ms, ls, masked):
        kext = jnp.concatenate([k_ref[j], kfeat], axis=1)
        vt = vt_ref[j]
        scores = [_dot(kext, qext[idx]) for idx in range(4)]
        new_ms, new_ls, alphas, probs = [], [], [], []
        for idx in range(4):
            h = idx // 2
            s = scores[idx]
            if masked:
                s = jnp.where(causal, s, NEG_BIG)
            shift = slopes[h] * ((i - j) * TQ).astype(F32)
            m_new = jnp.maximum(ms[idx], jnp.max(s, axis=0, keepdims=True) - shift)
            alphas.append(jnp.exp2(ms[idx] - m_new))
            p = jnp.exp2(s - (m_new + shift))
            new_ls.append(alphas[idx] * ls[idx] + jnp.sum(p, axis=0, keepdims=True))
            probs.append(p.astype(BF16))
            new_ms.append(m_new)
        outs = [_dot(vt, probs[idx]) for idx in range(4)]
        for idx in range(4):
            acc_ref[idx] = alphas[idx] * acc_ref[idx] + outs[idx]
        return tuple(new_ms), tuple(new_ls)

    acc_ref[...] = jnp.zeros_like(acc_ref)
    ms = (jnp.full((1, TQ), NEG_BIG, F32),) * 4
    ls = (jnp.zeros((1, TQ), F32),) * 4
    ms, ls = lax.fori_loop(0, i, lambda j, c: step(j, c[0], c[1], False), (ms, ls))
    ms, ls = step(i, ms, ls, True)

    head0 = drow < HEAD_DIM
    o_t = [acc_ref[2 * h] / ls[2 * h] - lam * (acc_ref[2 * h + 1] / ls[2 * h + 1]) for h in range(2)]
    out_t = jnp.where(head0, o_t[0], o_t[1])
    sq = out_t * out_t
    ss0 = jnp.sum(jnp.where(head0, sq, 0.0), axis=0, keepdims=True)
    ss1 = jnp.sum(jnp.where(head0, 0.0, sq), axis=0, keepdims=True)
    ms_t = jnp.where(head0, ss0, ss1) * (1.0 / HEAD_DIM)
    y_t = out_t * lax.rsqrt(ms_t + SUBLN_EPS) * g_ref[...] * (1.0 - LAMBDA_INIT)
    o_ref[...] = y_t.T.astype(BF16)


def _diff_prompt_attention(qt, kb4, vtb4, slopes, sfeat, kfeat, lam_vecs, g_col):
    b, _, s = qt.shape
    n_pairs = DF_WIDTH // LANES
    const2 = lambda bi, p, i: (0, 0)
    return pl.pallas_call(
        _diff_prompt_kernel,
        out_shape=jax.ShapeDtypeStruct((b, s, DF_WIDTH), BF16),
        grid=(b, n_pairs, s // TQ),
        in_specs=_prompt_specs(s, SB_WIDTH // LANES) + [
            pl.BlockSpec((None, 1, LANES), lambda bi, p, i: (p, 0, 0)),
            pl.BlockSpec((None, 2, LANES, TQ), lambda bi, p, i: (p, 0, 0, 0)),
            pl.BlockSpec((TQ, LANES), const2),
            pl.BlockSpec((4, DIFF_QK_DIM), const2),
            pl.BlockSpec((LANES, 1), const2),
        ],
        out_specs=pl.BlockSpec((None, TQ, LANES), lambda bi, p, i: (bi, i, p)),
        scratch_shapes=[pltpu.VMEM((4, LANES, TQ), F32)],
        compiler_params=pltpu.CompilerParams(
            dimension_semantics=("arbitrary", "arbitrary", "arbitrary"),
            vmem_limit_bytes=VMEM_LIMIT),
        name="diff_prompt_attn",
    )(qt, kb4, vtb4, slopes, sfeat, kfeat, lam_vecs, g_col)


PAGES_PER_STEP = 8
DEC_T = 8
SB_ROWS = N_SB_HEADS * DEC_T
DF_ROWS = N_DIFF_HEADS * 2 * DEC_T


def _group_rms_lanes(o, g, lane, n_groups):
    sq = o * o
    ms = jnp.zeros_like(o)
    for h in range(n_groups):
        in_h = (lane >= HEAD_DIM * h) & (lane < HEAD_DIM * (h + 1))
        s_h = jnp.sum(jnp.where(in_h, sq, 0.0), axis=1, keepdims=True)
        ms = jnp.where(in_h, s_h * (1.0 / HEAD_DIM), ms)
    return o * lax.rsqrt(ms + SUBLN_EPS) * g


def _dec_attn_kernel(pt_ref, q_ref, kn_ref, vn_ref, *rest, past_len):
    del pt_ref
    kp_refs = rest[:PAGES_PER_STEP]
    vp_refs = rest[PAGES_PER_STEP:2 * PAGES_PER_STEP]
    (trio_ref, slope_ref, lam_ref, g_ref, o_ref, wsb_ref, wdf_ref,
     run_ref, accsb_ref, mcol_ref, mrep_ref, l_ref, accdf_ref) = rest[2 * PAGES_PER_STEP:]
    step_id = pl.program_id(1)
    n_steps = pl.num_programs(1)

    key_lane = lax.broadcasted_iota(jnp.int32, (DF_ROWS, PAGE_SIZE), 1)
    t_df = lax.broadcasted_iota(jnp.int32, (DF_ROWS, PAGE_SIZE), 0) & (DEC_T - 1)
    slope = slope_ref[...]
    bias0 = slope * (key_lane - past_len - t_df).astype(F32)
    trio = trio_ref[...]

    def sb_group(chunks):
        offset = run_ref[...]
        acc = accsb_ref[...]
        parts = []
        for z, _, mask in chunks:
            sp = _softplus2(z)
            log_sig = z - sp
            if mask is not None:
                sp = jnp.where(mask, sp, 0.0)
            parts.append((log_sig, _split_bf16(sp)))
        sums = [_dot(hi, trio) + _dot(lo, trio) for _, (hi, lo) in parts]
        for (_, pv, mask), (log_sig, _), sm in zip(chunks, parts, sums):
            w = jnp.exp2(log_sig - sm[:, :PAGE_SIZE] - offset)
            if mask is not None:
                w = jnp.where(mask, w, 0.0)
            acc = acc + pv(w.astype(BF16))
            offset = offset + sm[:, PAGE_SIZE:]
        run_ref[...] = offset
        accsb_ref[...] = acc

    def df_group(chunks):
        top = chunks[0][0]
        for s, _ in chunks[1:]:
            top = jnp.maximum(top, s)
        m_old = mcol_ref[...]
        m_new = jnp.maximum(m_old, jnp.max(top, axis=1, keepdims=True))
        m_rep_old = mrep_ref[...]
        m_rep = jnp.broadcast_to(m_new, (DF_ROWS, PAGE_SIZE))
        p_sum = jnp.zeros((DF_ROWS, PAGE_SIZE), F32)
        pv_sum = jnp.zeros((DF_ROWS, DF_WIDTH), F32)
        for s, pv in chunks:
            p = jnp.exp2(s - m_rep)
            p_sum = p_sum + p
            pv_sum = pv_sum + pv(p.astype(BF16))
        alpha_rep = jnp.exp2(m_rep_old - m_rep)
        l_ref[...] = jnp.exp2(m_old - m_new) * l_ref[...] + jnp.sum(p_sum, axis=1, keepdims=True)
        accdf_ref[...] = jnp.concatenate([alpha_rep] * (DF_WIDTH // PAGE_SIZE), axis=1) * accdf_ref[...] + pv_sum
        mcol_ref[...] = m_new
        mrep_ref[...] = m_rep

    @pl.when(step_id == 0)
    def _init():
        qf = q_ref[...].astype(F32)
        q_sb = jnp.concatenate([qf[:, :SB_WIDTH]] * N_SB_HEADS, axis=0)
        r = lax.broadcasted_iota(jnp.int32, (SB_ROWS, SB_WIDTH), 0)
        c = lax.broadcasted_iota(jnp.int32, (SB_ROWS, SB_WIDTH), 1)
        wsb_ref[...] = jnp.where((r >> 3) == (c >> 6), q_sb, 0.0).astype(BF16)
        q_df = jnp.concatenate([qf[:, SB_WIDTH:]] * (2 * N_DIFF_HEADS), axis=0)
        r = lax.broadcasted_iota(jnp.int32, (DF_ROWS, DF_WIDTH), 0)
        c = lax.broadcasted_iota(jnp.int32, (DF_ROWS, DF_WIDTH), 1)
        wdf_ref[...] = jnp.where((r >> 3) == (c >> 5), q_df, 0.0).astype(BF16)
        run_ref[...] = jnp.zeros_like(run_ref)
        accsb_ref[...] = jnp.zeros_like(accsb_ref)
        mcol_ref[...] = jnp.full_like(mcol_ref, NEG_BIG)
        mrep_ref[...] = jnp.full_like(mrep_ref, NEG_BIG)
        l_ref[...] = jnp.zeros_like(l_ref)
        accdf_ref[...] = jnp.zeros_like(accdf_ref)
        pad = jnp.zeros((PAGE_SIZE - DEC_T, D_MODEL), F32)
        kn = jnp.concatenate([kn_ref[...], pad], axis=0).astype(BF16)
        vn = jnp.concatenate([vn_ref[...], pad], axis=0).astype(BF16)
        key_sb = lax.broadcasted_iota(jnp.int32, (SB_ROWS, PAGE_SIZE), 1)
        t_sb = lax.broadcasted_iota(jnp.int32, (SB_ROWS, PAGE_SIZE), 0) & (DEC_T - 1)
        sb_group([(_dot_nt(wsb_ref[...], kn[:, :SB_WIDTH]),
                   lambda w: _dot(w, vn[:, :SB_WIDTH]), key_sb < t_sb)])
        s_new = _dot_nt(wdf_ref[...], kn[:, SB_WIDTH:]) + (bias0 + slope * float(past_len))
        df_group([(jnp.where(key_lane <= t_df, s_new, NEG_BIG), lambda p: _dot(p, vn[:, SB_WIDTH:]))])

    n_pages = n_steps * PAGES_PER_STEP
    wsb = wsb_ref[...]
    wdf = wdf_ref[...]
    sb_chunks, df_chunks = [], []
    for r in range(PAGES_PER_STEP):
        page = n_pages - 1 - (step_id * PAGES_PER_STEP + r)
        kp = kp_refs[r][...].astype(BF16)
        vp = vp_refs[r][...].astype(BF16)
        sb_chunks.append((_dot(wsb, kp[:SB_WIDTH, :]),
                          functools.partial(_dot_nt, b=vp[:SB_WIDTH, :]), None))
        base = (page * PAGE_SIZE).astype(F32)
        df_chunks.append((_dot(wdf, kp[SB_WIDTH:, :]) + (bias0 + slope * base),
                          functools.partial(_dot_nt, b=vp[SB_WIDTH:, :])))
    sb_group(sb_chunks)
    df_group(df_chunks)

    @pl.when(step_id == n_steps - 1)
    def _finish():
        lane = lax.broadcasted_iota(jnp.int32, (DEC_T, SB_WIDTH), 1)
        lam = _lambda_value(lam_ref)
        acc_sb = accsb_ref[...]
        acc_df = accdf_ref[...] / l_ref[...]
        o_sb = jnp.zeros((DEC_T, SB_WIDTH), F32)
        o_df = jnp.zeros((DEC_T, DF_WIDTH), F32)
        for h in range(N_SB_HEADS):
            in_h = (lane >> 6) == h
            o_sb = jnp.where(in_h, acc_sb[DEC_T * h:DEC_T * (h + 1), :], o_sb)
            r0 = 2 * DEC_T * h
            o_h = acc_df[r0:r0 + DEC_T, :] - lam * acc_df[r0 + DEC_T:r0 + 2 * DEC_T, :]
            o_df = jnp.where(in_h, o_h, o_df)
        y_df = _group_rms_lanes(o_df, g_ref[...], lane, N_DIFF_HEADS) * (1.0 - LAMBDA_INIT)
        o_ref[:, :SB_WIDTH] = o_sb.astype(BF16)
        o_ref[:, SB_WIDTH:] = y_df.astype(BF16)


def _dec_attention(page_table, q, k_new, v_new, cache_k, cache_v, trio, slope_rep, lam_vecs, g8):
    n_seq, n_pages = page_table.shape
    past_len = n_pages * PAGE_SIZE
    n_steps = n_pages // PAGES_PER_STEP

    def page_spec(r):
        def idx(b, s, pt):
            return (pt[b, n_pages - 1 - (s * PAGES_PER_STEP + r)], 0, 0)
        return pl.BlockSpec((None, D_MODEL, PAGE_SIZE), idx)

    seq_spec = pl.BlockSpec((None, DEC_T, D_MODEL), lambda b, s, pt: (b, 0, 0))
    const2 = lambda b, s, pt: (0, 0)
    grid_spec = pltpu.PrefetchScalarGridSpec(
        num_scalar_prefetch=1,
        grid=(n_seq, n_steps),
        in_specs=[seq_spec, seq_spec, seq_spec]
        + [page_spec(r) for r in range(PAGES_PER_STEP)]
        + [page_spec(r) for r in range(PAGES_PER_STEP)]
        + [
            pl.BlockSpec((PAGE_SIZE, 2 * PAGE_SIZE), const2),
            pl.BlockSpec((DF_ROWS, PAGE_SIZE), const2),
            pl.BlockSpec((4, DIFF_QK_DIM), const2),
            pl.BlockSpec((1, DF_WIDTH), const2),
        ],
        out_specs=seq_spec,
        scratch_shapes=[
            pltpu.VMEM((SB_ROWS, SB_WIDTH), BF16),
            pltpu.VMEM((DF_ROWS, DF_WIDTH), BF16),
            pltpu.VMEM((SB_ROWS, PAGE_SIZE), F32),
            pltpu.VMEM((SB_ROWS, SB_WIDTH), F32),
            pltpu.VMEM((DF_ROWS, 1), F32),
            pltpu.VMEM((DF_ROWS, PAGE_SIZE), F32),
            pltpu.VMEM((DF_ROWS, 1), F32),
            pltpu.VMEM((DF_ROWS, DF_WIDTH), F32),
        ],
    )
    return pl.pallas_call(
        functools.partial(_dec_attn_kernel, past_len=past_len),
        out_shape=jax.ShapeDtypeStruct((n_seq, DEC_T, D_MODEL), BF16),
        grid_spec=grid_spec,
        compiler_params=pltpu.CompilerParams(
            dimension_semantics=("arbitrary", "arbitrary"), vmem_limit_bytes=VMEM_LIMIT),
        name="dec_attn",
    )(page_table, q, k_new, v_new, *([cache_k] * PAGES_PER_STEP), *([cache_v] * PAGES_PER_STEP),
      trio, slope_rep, lam_vecs, g8)


def _attn_out_kernel(ma_ref, mb_ref, x_ref, wa_ref, wb_ref, gpost_ref, gt_ref,
                     gpre_ref, sc_ref, sh_ref, x1_ref, h_ref):
    y = _dot(ma_ref[...], wa_ref[...]) + _dot(mb_ref[...], wb_ref[...])
    x1 = x_ref[...] + gt_ref[...] * _rms(y, gpost_ref[...], NORM_EPS)
    x1_ref[...] = x1
    h = _rms(x1, gpre_ref[...], NORM_EPS) * (1.0 + sc_ref[...]) + sh_ref[...]
    h_ref[...] = h.astype(BF16)


def _attn_out(mixed_a, a_col, mixed_b, b_col, x2d, w_out_bf16, g_post, gt, g_pre, sc, sh,
              tm, tiles_per_group):
    n = x2d.shape[0]
    half = D_MODEL // 2
    row_spec = pl.BlockSpec((tm, D_MODEL), lambda i: (i, 0))
    const2 = lambda i: (0, 0)
    vec_spec = pl.BlockSpec((1, D_MODEL), const2)
    mod = lambda a: _mod_spec(a.shape[1], tiles_per_group)
    return pl.pallas_call(
        _attn_out_kernel,
        out_shape=(jax.ShapeDtypeStruct((n, D_MODEL), F32),
                   jax.ShapeDtypeStruct((n, D_MODEL), BF16)),
        grid=(n // tm,),
        in_specs=[
            pl.BlockSpec((tm, half), lambda i: (i, a_col)),
            pl.BlockSpec((tm, half), lambda i: (i, b_col)),
            row_spec,
            pl.BlockSpec((half, D_MODEL), lambda i: (0, 0)),
            pl.BlockSpec((half, D_MODEL), lambda i: (1, 0)),
            vec_spec, mod(gt), vec_spec, mod(sc), mod(sh),
        ],
        out_specs=(row_spec, row_spec),
        compiler_params=pltpu.CompilerParams(
            dimension_semantics=("arbitrary",), vmem_limit_bytes=VMEM_LIMIT),
        name="attn_out",
    )(mixed_a, mixed_b, x2d, w_out_bf16, w_out_bf16, g_post.reshape(1, D_MODEL), gt,
      g_pre.reshape(1, D_MODEL), sc, sh)


UP_CHUNK = 1408


def _up_conv_kernel(h_ref, prev_ref, wup_ref, cw_ref, cb_ref, g_ref, st_ref, carry_ref, *, tm):
    t = pl.program_id(1)

    @pl.when(t == 0)
    def _load_state():
        carry_ref[...] = prev_ref[...]

    h = h_ref[...]
    row = lax.broadcasted_iota(jnp.int32, (tm, 1), 0)

    def conv_cols(c0):
        u = _dot(h, wup_ref[:, c0:c0 + UP_CHUNK])
        p0 = carry_ref[0:1, c0:c0 + UP_CHUNK]
        p1 = carry_ref[1:2, c0:c0 + UP_CHUNK]
        u1 = jnp.where(row == 0, p1, pltpu.roll(u, 1, 0))
        u2 = jnp.where(row == 0, p0, jnp.where(row == 1, p1, pltpu.roll(u, 2, 0)))
        w0 = cw_ref[0:1, c0:c0 + UP_CHUNK]
        w1 = cw_ref[1:2, c0:c0 + UP_CHUNK]
        w2 = cw_ref[2:3, c0:c0 + UP_CHUNK]
        conv = cb_ref[:, c0:c0 + UP_CHUNK] + w0 * u2 + w1 * u1 + w2 * u
        last = u[tm - 2:tm, :]
        carry_ref[:, c0:c0 + UP_CHUNK] = last
        st_ref[:, c0:c0 + UP_CHUNK] = last
        return conv

    for ch in range(D_FF // UP_CHUNK):
        a = conv_cols(ch * UP_CHUNK)
        b = conv_cols(D_FF + ch * UP_CHUNK)
        gate = a * (1.0 / (1.0 + jnp.exp(-a))) * b
        g_ref[:, ch * UP_CHUNK:(ch + 1) * UP_CHUNK] = gate.astype(BF16)


def _up_conv(h3d, conv_prev, w_up_bf16, conv_w, conv_b, tm):
    nb, t_len, _ = h3d.shape
    const2 = lambda b, t: (0, 0)
    return pl.pallas_call(
        functools.partial(_up_conv_kernel, tm=tm),
        out_shape=(jax.ShapeDtypeStruct((nb, t_len, D_FF), BF16),
                   jax.ShapeDtypeStruct((nb, CONV_WIDTH - 1, 2 * D_FF), F32)),
        grid=(nb, t_len // tm),
        in_specs=[
            pl.BlockSpec((None, tm, D_MODEL), lambda b, t: (b, t, 0)),
            pl.BlockSpec((None, CONV_WIDTH - 1, 2 * D_FF), lambda b, t: (b, 0, 0)),
            pl.BlockSpec((D_MODEL, 2 * D_FF), const2),
            pl.BlockSpec((CONV_WIDTH, 2 * D_FF), const2),
            pl.BlockSpec((1, 2 * D_FF), const2),
        ],
        out_specs=(pl.BlockSpec((None, tm, D_FF), lambda b, t: (b, t, 0)),
                   pl.BlockSpec((None, CONV_WIDTH - 1, 2 * D_FF), lambda b, t: (b, 0, 0))),
        scratch_shapes=[pltpu.VMEM((CONV_WIDTH - 1, 2 * D_FF), F32)],
        compiler_params=pltpu.CompilerParams(
            dimension_semantics=("arbitrary", "arbitrary"), vmem_limit_bytes=VMEM_LIMIT),
        name="up_conv_gate",
    )(h3d, conv_prev, w_up_bf16, conv_w, conv_b.reshape(1, 2 * D_FF))


def _down_kernel(g_ref, x1_ref, wd_ref, gpost_ref, gt_ref, y_ref):
    f = _dot(g_ref[...], wd_ref[...])
    y_ref[...] = x1_ref[...] + gt_ref[...] * _rms(f, gpost_ref[...], NORM_EPS)


def _down_proj(g2d, x1, w_down_bf16, g_post, gt, tm, tiles_per_group):
    n = x1.shape[0]
    row_spec = pl.BlockSpec((tm, D_MODEL), lambda i: (i, 0))
    const2 = lambda i: (0, 0)
    return pl.pallas_call(
        _down_kernel,
        out_shape=jax.ShapeDtypeStruct((n, D_MODEL), F32),
        grid=(n // tm,),
        in_specs=[
            pl.BlockSpec((tm, D_FF), lambda i: (i, 0)),
            row_spec,
            pl.BlockSpec((D_FF, D_MODEL), const2),
            pl.BlockSpec((1, D_MODEL), const2),
            _mod_spec(gt.shape[1], tiles_per_group),
        ],
        out_specs=row_spec,
        compiler_params=pltpu.CompilerParams(
            dimension_semantics=("arbitrary",), vmem_limit_bytes=VMEM_LIMIT),
        name="down_proj",
    )(g2d, x1, w_down_bf16, g_post.reshape(1, D_MODEL), gt)


def _later_mask(n):
    idx = np.arange(n)
    return (idx[None, :] > idx[:, None]).astype(np.float32)


def _alibi_slopes_np():
    return (2.0 ** (-(8.0 / N_DIFF_HEADS) * np.arange(1, N_DIFF_HEADS + 1))).astype(np.float32)


def _alibi_features(slopes2):
    bf = ml_dtypes.bfloat16
    hi = slopes2.astype(bf).astype(np.float32)
    mid = (slopes2 - hi).astype(bf).astype(np.float32)
    lo = (slopes2 - hi - mid).astype(bf).astype(np.float32)
    sfeat = np.zeros((N_DIFF_HEADS, LANES, TQ), np.float32)
    for row, piece in enumerate((hi, mid, lo)):
        sfeat[:, row, :] = piece[:, None]
    kfeat = np.zeros((TQ, LANES), np.float32)
    kfeat[:, :3] = np.arange(TQ, dtype=np.float32)[:, None]
    return (jnp.asarray(sfeat.reshape(N_DIFF_HEADS // 2, 2, LANES, TQ), dtype=BF16),
            jnp.asarray(kfeat, dtype=BF16))


def kernel(x_prompt, x_sample, c_prompt, c_sample, cache_k, cache_v, state_conv, page_table,
           w_ada, b_ada, g_pre_attn, g_post_attn, w_in, w_out, lambda_q1, lambda_k1,
           lambda_q2, lambda_k2, g_subln, g_pre_mlp, g_post_mlp, w_up, conv_w, conv_b, w_down):
    layer = 0
    n_b, seq, _ = x_prompt.shape
    n_dec, dec_t, _ = x_sample.shape
    assert dec_t == DEC_T
    n_pool = cache_k.shape[1]

    w_in_b = w_in[layer].astype(BF16)
    w_out_b = w_out[layer].astype(BF16)
    w_up_b = w_up[layer].astype(BF16)
    w_down_b = w_down[layer].astype(BF16)

    c_all = jnp.concatenate([c_prompt, c_sample], axis=0)
    mod = _modulation(c_all, w_ada[layer], b_ada[layer])
    mod_p = [m.reshape(n_b, 1, D_MODEL) for m in jnp.split(mod[:n_b], N_MOD, axis=-1)]
    mod_s = [jnp.repeat(m, DEC_T, axis=0).reshape(1, n_dec * DEC_T, D_MODEL)
             for m in jnp.split(mod[n_b:], N_MOD, axis=-1)]

    slopes2 = _alibi_slopes_np() * np.float32(LOG2E)
    slope_pairs = jnp.asarray(np.repeat(slopes2, HEAD_DIM).reshape(N_DIFF_HEADS // 2, 1, LANES))
    slope_rep = jnp.asarray(np.broadcast_to(np.repeat(slopes2, 2 * DEC_T)[:, None],
                                            (DF_ROWS, PAGE_SIZE)))
    sfeat, kfeat = _alibi_features(slopes2)
    lam_vecs = jnp.stack([lambda_q1[layer], lambda_k1[layer],
                          lambda_q2[layer], lambda_k2[layer]]).astype(F32)
    g_sub = g_subln[layer].astype(F32)
    g_col = jnp.tile(g_sub, 2).reshape(LANES, 1)
    g8 = jnp.tile(g_sub, N_DIFF_HEADS).reshape(1, DF_WIDTH)
    usuf = jnp.asarray(_later_mask(TQ), dtype=BF16)
    tri = _later_mask(PAGE_SIZE).T
    trio = jnp.asarray(np.concatenate([tri, np.ones_like(tri)], axis=1), dtype=BF16)

    tm_p = TQ
    tiles_p = seq // tm_p
    xp2d = x_prompt.reshape(n_b * seq, D_MODEL)
    sh_a, sc_a, gt_a, sh_m, sc_m, gt_m = mod_p
    qt_p, kt_p, vt_p, kb_p, vtb_p = _qkv_prompt(xp2d, sc_a, sh_a, g_pre_attn[layer], w_in_b,
                                                n_b, seq, tm_p)
    kb4 = kb_p.reshape(n_b, tiles_p, TQ, D_MODEL)
    mixed_sb = _sb_prompt_attention(qt_p, kb4, vtb_p, usuf)
    mixed_df = _diff_prompt_attention(qt_p, kb4, vtb_p, slope_pairs, sfeat, kfeat, lam_vecs, g_col)
    x1_p, h_p = _attn_out(mixed_sb.reshape(n_b * seq, SB_WIDTH), 0,
                          mixed_df.reshape(n_b * seq, DF_WIDTH), 0,
                          xp2d, w_out_b, g_post_attn[layer], gt_a, g_pre_mlp[layer], sc_m, sh_m,
                          tm_p, tiles_p)
    conv0 = jnp.zeros((n_b, CONV_WIDTH - 1, 2 * D_FF), F32)
    gate_p, conv_p = _up_conv(h_p.reshape(n_b, seq, D_MODEL), conv0, w_up_b,
                              conv_w[layer], conv_b[layer], tm_p)
    y_p = _down_proj(gate_p.reshape(n_b * seq, D_FF), x1_p, w_down_b, g_post_mlp[layer],
                     gt_m, tm_p, tiles_p)

    tm_s = n_dec * DEC_T
    xs2d = x_sample.reshape(tm_s, D_MODEL)
    sh_a, sc_a, gt_a, sh_m, sc_m, gt_m = mod_s
    q_s, k_s, v_s = _qkv_rows(xs2d, sc_a, sh_a, g_pre_attn[layer], w_in_b, tm_s)
    per_seq = lambda a: a.reshape(n_dec, DEC_T, D_MODEL)
    pages = lambda c: c[layer].transpose(0, 2, 3, 1).reshape(n_pool, D_MODEL, PAGE_SIZE)
    mixed_s = _dec_attention(
        page_table, per_seq(q_s), per_seq(k_s), per_seq(v_s), pages(cache_k), pages(cache_v),
        trio, slope_rep, lam_vecs, g8)
    mixed_s2d = mixed_s.reshape(tm_s, D_MODEL)
    x1_s, h_s = _attn_out(mixed_s2d, 0, mixed_s2d, 1, xs2d, w_out_b, g_post_attn[layer], gt_a,
                          g_pre_mlp[layer], sc_m, sh_m, tm_s, 1)
    gate_s, conv_s = _up_conv(h_s.reshape(n_dec, DEC_T, D_MODEL), state_conv[layer].astype(F32),
                              w_up_b, conv_w[layer], conv_b[layer], DEC_T)
    y_s = _down_proj(gate_s.reshape(tm_s, D_FF), x1_s, w_down_b, g_post_mlp[layer], gt_m, tm_s, 1)

    heads = lambda a, b, t: a.reshape(1, b, t, N_HEADS, HEAD_DIM)
    heads_t = lambda a: a.reshape(1, n_b, N_HEADS, HEAD_DIM, seq).transpose(0, 1, 4, 2, 3)
    return (y_p.reshape(n_b, seq, D_MODEL),
            y_s.reshape(n_dec, DEC_T, D_MODEL),
            heads_t(kt_p), heads_t(vt_p), conv_p[None],
            heads(k_s, n_dec, DEC_T), heads(v_s, n_dec, DEC_T), conv_s[None])
```

```python
import functools
import math

import jax
import jax.numpy as jnp
import ml_dtypes
import numpy as np
from jax import lax
from jax.experimental import pallas as pl
from jax.experimental.pallas import tpu as pltpu

F32 = jnp.float32
BF16 = jnp.bfloat16

D_MODEL = 1024
HEAD_DIM = 64
N_SB_HEADS = 8
N_DIFF_HEADS = 8
N_HEADS = N_SB_HEADS + N_DIFF_HEADS
SB_WIDTH = N_SB_HEADS * HEAD_DIM
DF_WIDTH = N_DIFF_HEADS * HEAD_DIM
DIFF_QK_DIM = HEAD_DIM // 2
D_FF = 2816
CONV_WIDTH = 3
PAGE_SIZE = 128
NORM_EPS = 1e-6
SUBLN_EPS = 1e-5
N_MOD = 6
LAMBDA_INIT = 0.8 - 0.6 * math.exp(-0.3 * 0)

LANES = 128
SUBLANES = 8
NEG_BIG = -1e30
LOG2E = math.log2(math.e)
SB_DEAD_LOG2 = 104.0 * LOG2E + 1.0

VMEM_LIMIT = 56 * 1024 * 1024


def _dot(a, b):
    return jnp.dot(a, b, preferred_element_type=F32)


def _dot_nt(a, b):
    return lax.dot_general(a, b, (((1,), (1,)), ((), ())), preferred_element_type=F32)


def _rms(x, g, eps):
    return x * lax.rsqrt(jnp.mean(x * x, axis=-1, keepdims=True) + eps) * g


def _softplus2(z):
    return jnp.maximum(z, 0.0) + jnp.log2(1.0 + jnp.exp2(-jnp.abs(z)))


def _split_bf16(x):
    hi = x.astype(BF16)
    lo = (x - hi.astype(F32)).astype(BF16)
    return hi, lo


def _mod_kernel(c_ref, w_ref, b_ref, o_ref):
    c = c_ref[...]
    s = c * (1.0 / (1.0 + jnp.exp(-c)))
    o_ref[...] = _dot(s.astype(BF16), w_ref[...].astype(BF16)) + b_ref[...]


def _modulation(c_all, w_ada, b_ada):
    n_rows = c_all.shape[0]
    n_out = w_ada.shape[1]
    tn = 1536
    return pl.pallas_call(
        _mod_kernel,
        out_shape=jax.ShapeDtypeStruct((n_rows, n_out), F32),
        grid=(n_out // tn,),
        in_specs=[
            pl.BlockSpec((n_rows, D_MODEL), lambda j: (0, 0)),
            pl.BlockSpec((D_MODEL, tn), lambda j: (0, j)),
            pl.BlockSpec((1, tn), lambda j: (0, j)),
        ],
        out_specs=pl.BlockSpec((n_rows, tn), lambda j: (0, j)),
        compiler_params=pltpu.CompilerParams(
            dimension_semantics=("arbitrary",), vmem_limit_bytes=VMEM_LIMIT),
        name="adaln_mod",
    )(c_all, w_ada, b_ada.reshape(1, n_out))


def _scaled_qkv(x_ref, sc_ref, sh_ref, g_ref, w_ref):
    x = x_ref[...]
    h = _rms(x, g_ref[...], NORM_EPS) * (1.0 + sc_ref[...]) + sh_ref[...]
    qkv = _dot(h.astype(BF16), w_ref[...])
    lane = lax.broadcasted_iota(jnp.int32, (1, D_MODEL), 1)
    qscale = jnp.where(lane < SB_WIDTH, LOG2E * HEAD_DIM ** -0.5,
                       LOG2E * DIFF_QK_DIM ** -0.5).astype(F32)
    return qkv[:, :D_MODEL] * qscale, qkv[:, D_MODEL:2 * D_MODEL], qkv[:, 2 * D_MODEL:]


def _qkv_rows_kernel(x_ref, sc_ref, sh_ref, g_ref, w_ref, q_ref, k_ref, v_ref):
    q, k, v = _scaled_qkv(x_ref, sc_ref, sh_ref, g_ref, w_ref)
    q_ref[...] = q.astype(BF16)
    k_ref[...] = k
    v_ref[...] = v


def _qkv_prompt_kernel(x_ref, sc_ref, sh_ref, g_ref, w_ref, qt_ref, kt_ref, vt_ref, kb_ref, vtb_ref):
    q, k, v = _scaled_qkv(x_ref, sc_ref, sh_ref, g_ref, w_ref)
    qt_ref[...] = q.T.astype(BF16)
    kt_ref[...] = k.T
    vt = v.T
    vt_ref[...] = vt
    vtb_ref[...] = vt.astype(BF16)
    kb_ref[...] = k.astype(BF16)


def _mod_spec(mod_rows, tiles_per_group):
    return pl.BlockSpec((None, mod_rows, D_MODEL), lambda i: (i // tiles_per_group, 0, 0))


def _qkv_in_specs(sc, sh, tm, tiles_per_group):
    const2 = lambda i: (0, 0)
    return [
        pl.BlockSpec((tm, D_MODEL), lambda i: (i, 0)),
        _mod_spec(sc.shape[1], tiles_per_group),
        _mod_spec(sh.shape[1], tiles_per_group),
        pl.BlockSpec((1, D_MODEL), const2),
        pl.BlockSpec((D_MODEL, 3 * D_MODEL), const2),
    ]


def _qkv_rows(x2d, sc, sh, g, w_bf16, tm):
    n = x2d.shape[0]
    row_spec = pl.BlockSpec((tm, D_MODEL), lambda i: (i, 0))
    out_f32 = jax.ShapeDtypeStruct((n, D_MODEL), F32)
    return pl.pallas_call(
        _qkv_rows_kernel,
        out_shape=(jax.ShapeDtypeStruct((n, D_MODEL), BF16), out_f32, out_f32),
        grid=(n // tm,),
        in_specs=_qkv_in_specs(sc, sh, tm, 1),
        out_specs=(row_spec, row_spec, row_spec),
        compiler_params=pltpu.CompilerParams(
            dimension_semantics=("arbitrary",), vmem_limit_bytes=VMEM_LIMIT),
        name="qkv_rows",
    )(x2d, sc, sh, g.reshape(1, D_MODEL), w_bf16)


def _qkv_prompt(x2d, sc, sh, g, w_bf16, n_b, seq, tm):
    tiles = seq // tm
    t_spec = pl.BlockSpec((None, D_MODEL, tm), lambda i: (i // tiles, 0, i % tiles))
    t_f32 = jax.ShapeDtypeStruct((n_b, D_MODEL, seq), F32)
    return pl.pallas_call(
        _qkv_prompt_kernel,
        out_shape=(jax.ShapeDtypeStruct((n_b, D_MODEL, seq), BF16), t_f32, t_f32,
                   jax.ShapeDtypeStruct((n_b * seq, D_MODEL), BF16),
                   jax.ShapeDtypeStruct((n_b, tiles, D_MODEL, tm), BF16)),
        grid=(n_b * tiles,),
        in_specs=_qkv_in_specs(sc, sh, tm, tiles),
        out_specs=(t_spec, t_spec, t_spec,
                   pl.BlockSpec((tm, D_MODEL), lambda i: (i, 0)),
                   pl.BlockSpec((None, None, D_MODEL, tm), lambda i: (i // tiles, i % tiles, 0, 0))),
        compiler_params=pltpu.CompilerParams(
            dimension_semantics=("arbitrary",), vmem_limit_bytes=VMEM_LIMIT),
        name="qkv_prompt",
    )(x2d, sc, sh, g.reshape(1, D_MODEL), w_bf16)


def _lambda_value(lam_ref):
    lq1 = lam_ref[0:1, :]
    lk1 = lam_ref[1:2, :]
    lq2 = lam_ref[2:3, :]
    lk2 = lam_ref[3:4, :]
    return (jnp.exp(jnp.sum(lq1 * lk1, axis=1, keepdims=True))
            - jnp.exp(jnp.sum(lq2 * lk2, axis=1, keepdims=True)) + LAMBDA_INIT)


TQ = 256
DIFF_GROUPS = (8, 4, 2, 1)


def _prompt_specs(s, first_pair):
    n_blk = s // TQ
    return [
        pl.BlockSpec((None, LANES, TQ), lambda bi, p, i: (bi, first_pair + p, i)),
        pl.BlockSpec((None, n_blk, TQ, LANES), lambda bi, p, i: (bi, 0, 0, first_pair + p)),
        pl.BlockSpec((None, n_blk, LANES, TQ), lambda bi, p, i: (bi, 0, first_pair + p, 0)),
    ]


def _sb_prompt_kernel(qt_ref, k_ref, vt_ref, usuf_ref, o_ref, acc_ref):
    i = pl.program_id(2)
    qt = qt_ref[...].astype(F32)
    drow = lax.broadcasted_iota(jnp.int32, (LANES, TQ), 0)
    key = lax.broadcasted_iota(jnp.int32, (TQ, TQ), 0)
    qry = lax.broadcasted_iota(jnp.int32, (TQ, TQ), 1)
    strictly_causal = key < qry
    usuf = usuf_ref[...]
    in_head = [(drow >= HEAD_DIM * h) & (drow < HEAD_DIM * (h + 1)) for h in range(2)]
    qtm = [jnp.where(in_head[h], qt, 0.0).astype(BF16) for h in range(2)]

    def step(js, runs, masked_first):
        streams = [(b, h) for b in range(len(js)) for h in range(2)]
        kbs = [k_ref[j] for j in js]
        vts = [vt_ref[j] for j in js]
        zs = {(b, h): _dot(kbs[b], qtm[h]) for b, h in streams}
        log_sig, parts, offsets = {}, {}, {}
        runs = list(runs)
        for b, h in streams:
            masked = masked_first and b == 0
            sp = _softplus2(zs[b, h])
            log_sig[b, h] = zs[b, h] - sp
            if masked:
                sp = jnp.where(strictly_causal, sp, 0.0)
            parts[b, h] = _split_bf16(sp)
            offsets[b, h] = runs[h]
            runs[h] = runs[h] + jnp.sum(sp, axis=0, keepdims=True)
        between = {s: _dot(usuf, parts[s][0]) + _dot(usuf, parts[s][1]) for s in streams}
        weights = {}
        for b, h in streams:
            w = jnp.exp2(log_sig[b, h] - between[b, h] - offsets[b, h])
            if masked_first and b == 0:
                w = jnp.where(strictly_causal, w, 0.0)
            weights[b, h] = w.astype(BF16)
        outs = {s: _dot(vts[s[0]], weights[s]) for s in streams}
        for h in range(2):
            acc = acc_ref[h]
            for b in range(len(js)):
                acc = acc + outs[b, h]
            acc_ref[h] = acc
        return tuple(runs)

    def alive(runs):
        return (jnp.min(jnp.minimum(runs[0], runs[1])) < SB_DEAD_LOG2).astype(jnp.int32)

    acc_ref[...] = jnp.zeros_like(acc_ref)
    zero = jnp.zeros((1, TQ), F32)
    runs = lax.cond(i > 0,
                    lambda: step([i, i - 1], (zero, zero), True),
                    lambda: step([i], (zero, zero), True))

    def cond(c):
        return (c[0] < i) & (c[1] > 0)

    def body(c):
        runs = step([i - 1 - c[0]], (c[2], c[3]), False)
        return c[0] + 1, alive(runs), runs[0], runs[1]

    lax.while_loop(cond, body, (jnp.int32(1), alive(runs), runs[0], runs[1]))
    out_t = jnp.where(in_head[0], acc_ref[0], acc_ref[1])
    o_ref[...] = out_t.T.astype(BF16)


def _sb_prompt_attention(qt, kb4, vtb4, usuf):
    b, _, s = qt.shape
    n_pairs = SB_WIDTH // LANES
    return pl.pallas_call(
        _sb_prompt_kernel,
        out_shape=jax.ShapeDtypeStruct((b, s, SB_WIDTH), BF16),
        grid=(b, n_pairs, s // TQ),
        in_specs=_prompt_specs(s, 0) + [pl.BlockSpec((TQ, TQ), lambda bi, p, i: (0, 0))],
        out_specs=pl.BlockSpec((None, TQ, LANES), lambda bi, p, i: (bi, i, p)),
        scratch_shapes=[pltpu.VMEM((2, LANES, TQ), F32)],
        compiler_params=pltpu.CompilerParams(
            dimension_semantics=("arbitrary", "arbitrary", "arbitrary"),
            vmem_limit_bytes=VMEM_LIMIT),
        name="sb_prompt_attn",
    )(qt, kb4, vtb4, usuf)


def _diff_prompt_kernel(qt_ref, k_ref, vt_ref, slope_ref, sfeat_ref, kfeat_ref, lam_ref, g_ref,
                        o_ref, acc_ref):
    i = pl.program_id(2)
    qt = qt_ref[...].astype(F32)
    drow = lax.broadcasted_iota(jnp.int32, (LANES, TQ), 0)
    key = lax.broadcasted_iota(jnp.int32, (TQ, TQ), 0)
    qry = lax.broadcasted_iota(jnp.int32, (TQ, TQ), 1)
    causal = key <= qry
    lam = _lambda_value(lam_ref)
    slopes = [slope_ref[:, HEAD_DIM * h:HEAD_DIM * h + 1] for h in range(2)]
    kfeat = kfeat_ref[...]
    qext = []
    for h in range(2):
        for c in range(2):
            lo_row = HEAD_DIM * h + DIFF_QK_DIM * c
            in_map = (drow >= lo_row) & (drow < lo_row + DIFF_QK_DIM)
            qtm = jnp.where(in_map, qt, 0.0).astype(BF16)
            qext.append(jnp.concatenate([qtm, sfeat_ref[h]], axis=0))

    head_rows = [(drow >= HEAD_DIM * h) & (drow < HEAD_DIM * (h + 1)) for h in range(2)]
    keep = [jnp.where(head_rows[h], 1.0, 0.0).astype(BF16) for h in range(2)]
    fill = [jnp.where(head_rows[h], 0.0, 1.0).astype(BF16) for h in range(2)]

    def step(js, ms, masked):
        ms = list(ms)
        kexts = [jnp.concatenate([k_ref[j], kfeat], axis=1) for j in js]
        scores = [[_dot(kext, qext[idx]) for idx in range(4)] for kext in kexts]
        pending = []
        for b, j in enumerate(js):
            alphas, probs = [], []
            for idx in range(4):
                h = idx // 2
                s = scores[b][idx]
                if masked:
                    s = jnp.where(causal, s, NEG_BIG)
                shift = slopes[h] * ((i - j) * TQ).astype(F32)
                m_new = jnp.maximum(ms[idx], jnp.max(s, axis=0, keepdims=True) - shift)
                alphas.append(jnp.exp2(ms[idx] - m_new))
                probs.append(jnp.exp2(s - (m_new + shift)).astype(BF16))
                ms[idx] = m_new
            vt = vt_ref[j]
            vts = [vt * keep[h] + fill[h] for h in range(2)]
            pending.append((alphas, [_dot(vts[idx // 2], probs[idx]) for idx in range(4)]))
        for idx in range(4):
            acc = acc_ref[idx]
            for alphas, outs in pending:
                acc = alphas[idx] * acc + outs[idx]
            acc_ref[idx] = acc
        return tuple(ms)

    acc_ref[...] = jnp.zeros_like(acc_ref)
    ms = (jnp.full((1, TQ), NEG_BIG, F32),) * 4
    done = jnp.int32(0)
    for group in DIFF_GROUPS:
        n_trips = (i - done) // group
        ms = lax.fori_loop(
            0, n_trips,
            lambda t, m, done=done, group=group: step(
                [done + t * group + u for u in range(group)], m, False),
            ms)
        done = done + n_trips * group
    step([i], ms, True)

    head0 = drow < HEAD_DIM
    o_t = []
    for h in range(2):
        row = HEAD_DIM * (1 - h)
        maps = [acc_ref[2 * h + c] / acc_ref[2 * h + c, row:row + 1, :] for c in range(2)]
        o_t.append(maps[0] - lam * maps[1])
    out_t = jnp.where(head0, o_t[0], o_t[1])
    sq = out_t * out_t
    ss0 = jnp.sum(jnp.where(head0, sq, 0.0), axis=0, keepdims=True)
    ss1 = jnp.sum(jnp.where(head0, 0.0, sq), axis=0, keepdims=True)
    ms_t = jnp.where(head0, ss0, ss1) * (1.0 / HEAD_DIM)
    y_t = out_t * lax.rsqrt(ms_t + SUBLN_EPS) * g_ref[...] * (1.0 - LAMBDA_INIT)
    o_ref[...] = y_t.T.astype(BF16)


def _diff_prompt_attention(qt, kb4, vtb4, slopes, sfeat, kfeat, lam_vecs, g_col):
    b, _, s = qt.shape
    n_pairs = DF_WIDTH // LANES
    const2 = lambda bi, p, i: (0, 0)
    return pl.pallas_call(
        _diff_prompt_kernel,
        out_shape=jax.ShapeDtypeStruct((b, s, DF_WIDTH), BF16),
        grid=(b, n_pairs, s // TQ),
        in_specs=_prompt_specs(s, SB_WIDTH // LANES) + [
            pl.BlockSpec((None, 1, LANES), lambda bi, p, i: (p, 0, 0)),
            pl.BlockSpec((None, 2, LANES, TQ), lambda bi, p, i: (p, 0, 0, 0)),
            pl.BlockSpec((TQ, LANES), const2),
            pl.BlockSpec((4, DIFF_QK_DIM), const2),
            pl.BlockSpec((LANES, 1), const2),
        ],
        out_specs=pl.BlockSpec((None, TQ, LANES), lambda bi, p, i: (bi, i, p)),
        scratch_shapes=[pltpu.VMEM((4, LANES, TQ), F32)],
        compiler_params=pltpu.CompilerParams(
            dimension_semantics=("arbitrary", "arbitrary", "arbitrary"),
            vmem_limit_bytes=VMEM_LIMIT),
        name="diff_prompt_attn",
    )(qt, kb4, vtb4, slopes, sfeat, kfeat, lam_vecs, g_col)


PAGES_PER_STEP = 8
DEC_T = 8
SB_ROWS = N_SB_HEADS * DEC_T
DF_ROWS = N_DIFF_HEADS * 2 * DEC_T


def _group_rms_lanes(o, g, lane, n_groups):
    sq = o * o
    ms = jnp.zeros_like(o)
    for h in range(n_groups):
        in_h = (lane >= HEAD_DIM * h) & (lane < HEAD_DIM * (h + 1))
        s_h = jnp.sum(jnp.where(in_h, sq, 0.0), axis=1, keepdims=True)
        ms = jnp.where(in_h, s_h * (1.0 / HEAD_DIM), ms)
    return o * lax.rsqrt(ms + SUBLN_EPS) * g


def _dec_attn_kernel(pt_ref, q_ref, kn_ref, vn_ref, *rest, past_len):
    del pt_ref
    kp_refs = rest[:PAGES_PER_STEP]
    vp_refs = rest[PAGES_PER_STEP:2 * PAGES_PER_STEP]
    (trio_ref, slope_ref, lam_ref, g_ref, o_ref, wsb_ref, wdf_ref,
     run_ref, accsb_ref, mcol_ref, mrep_ref, l_ref, accdf_ref) = rest[2 * PAGES_PER_STEP:]
    step_id = pl.program_id(1)
    n_steps = pl.num_programs(1)

    key_lane = lax.broadcasted_iota(jnp.int32, (DF_ROWS, PAGE_SIZE), 1)
    t_df = lax.broadcasted_iota(jnp.int32, (DF_ROWS, PAGE_SIZE), 0) & (DEC_T - 1)
    slope = slope_ref[...]
    bias0 = slope * (key_lane - past_len - t_df).astype(F32)
    trio = trio_ref[...]

    def sb_group(chunks):
        offset = run_ref[...]
        acc = accsb_ref[...]
        parts = []
        for z, _, mask in chunks:
            sp = _softplus2(z)
            log_sig = z - sp
            if mask is not None:
                sp = jnp.where(mask, sp, 0.0)
            parts.append((log_sig, _split_bf16(sp)))
        sums = [_dot(hi, trio) + _dot(lo, trio) for _, (hi, lo) in parts]
        for (_, pv, mask), (log_sig, _), sm in zip(chunks, parts, sums):
            w = jnp.exp2(log_sig - sm[:, :PAGE_SIZE] - offset)
            if mask is not None:
                w = jnp.where(mask, w, 0.0)
            acc = acc + pv(w.astype(BF16))
            offset = offset + sm[:, PAGE_SIZE:]
        run_ref[...] = offset
        accsb_ref[...] = acc

    def df_group(chunks):
        top = chunks[0][0]
        for s, _ in chunks[1:]:
            top = jnp.maximum(top, s)
        m_old = mcol_ref[...]
        m_new = jnp.maximum(m_old, jnp.max(top, axis=1, keepdims=True))
        m_rep_old = mrep_ref[...]
        m_rep = jnp.broadcast_to(m_new, (DF_ROWS, PAGE_SIZE))
        p_sum = jnp.zeros((DF_ROWS, PAGE_SIZE), F32)
        pv_sum = jnp.zeros((DF_ROWS, DF_WIDTH), F32)
        for s, pv in chunks:
            p = jnp.exp2(s - m_rep)
            p_sum = p_sum + p
            pv_sum = pv_sum + pv(p.astype(BF16))
        alpha_rep = jnp.exp2(m_rep_old - m_rep)
        l_ref[...] = jnp.exp2(m_old - m_new) * l_ref[...] + jnp.sum(p_sum, axis=1, keepdims=True)
        accdf_ref[...] = jnp.concatenate([alpha_rep] * (DF_WIDTH // PAGE_SIZE), axis=1) * accdf_ref[...] + pv_sum
        mcol_ref[...] = m_new
        mrep_ref[...] = m_rep

    @pl.when(step_id == 0)
    def _init():
        qf = q_ref[...].astype(F32)
        q_sb = jnp.concatenate([qf[:, :SB_WIDTH]] * N_SB_HEADS, axis=0)
        r = lax.broadcasted_iota(jnp.int32, (SB_ROWS, SB_WIDTH), 0)
        c = lax.broadcasted_iota(jnp.int32, (SB_ROWS, SB_WIDTH), 1)
        wsb_ref[...] = jnp.where((r >> 3) == (c >> 6), q_sb, 0.0).astype(BF16)
        q_df = jnp.concatenate([qf[:, SB_WIDTH:]] * (2 * N_DIFF_HEADS), axis=0)
        r = lax.broadcasted_iota(jnp.int32, (DF_ROWS, DF_WIDTH), 0)
        c = lax.broadcasted_iota(jnp.int32, (DF_ROWS, DF_WIDTH), 1)
        wdf_ref[...] = jnp.where((r >> 3) == (c >> 5), q_df, 0.0).astype(BF16)
        run_ref[...] = jnp.zeros_like(run_ref)
        accsb_ref[...] = jnp.zeros_like(accsb_ref)
        mcol_ref[...] = jnp.full_like(mcol_ref, NEG_BIG)
        mrep_ref[...] = jnp.full_like(mrep_ref, NEG_BIG)
        l_ref[...] = jnp.zeros_like(l_ref)
        accdf_ref[...] = jnp.zeros_like(accdf_ref)
        pad = jnp.zeros((PAGE_SIZE - DEC_T, D_MODEL), F32)
        kn = jnp.concatenate([kn_ref[...], pad], axis=0).astype(BF16)
        vn = jnp.concatenate([vn_ref[...], pad], axis=0).astype(BF16)
        key_sb = lax.broadcasted_iota(jnp.int32, (SB_ROWS, PAGE_SIZE), 1)
        t_sb = lax.broadcasted_iota(jnp.int32, (SB_ROWS, PAGE_SIZE), 0) & (DEC_T - 1)
        sb_group([(_dot_nt(wsb_ref[...], kn[:, :SB_WIDTH]),
                   lambda w: _dot(w, vn[:, :SB_WIDTH]), key_sb < t_sb)])
        s_new = _dot_nt(wdf_ref[...], kn[:, SB_WIDTH:]) + (bias0 + slope * float(past_len))
        df_group([(jnp.where(key_lane <= t_df, s_new, NEG_BIG), lambda p: _dot(p, vn[:, SB_WIDTH:]))])

    n_pages = n_steps * PAGES_PER_STEP
    wsb = wsb_ref[...]
    wdf = wdf_ref[...]
    sb_chunks, df_chunks = [], []
    for r in range(PAGES_PER_STEP):
        page = n_pages - 1 - (step_id * PAGES_PER_STEP + r)
        kp = kp_refs[r][...].astype(BF16)
        vp = vp_refs[r][...].astype(BF16)
        sb_chunks.append((_dot(wsb, kp[:SB_WIDTH, :]),
                          functools.partial(_dot_nt, b=vp[:SB_WIDTH, :]), None))
        base = (page * PAGE_SIZE).astype(F32)
        df_chunks.append((_dot(wdf, kp[SB_WIDTH:, :]) + (bias0 + slope * base),
                          functools.partial(_dot_nt, b=vp[SB_WIDTH:, :])))
    sb_group(sb_chunks)
    df_group(df_chunks)

    @pl.when(step_id == n_steps - 1)
    def _finish():
        lane = lax.broadcasted_iota(jnp.int32, (DEC_T, SB_WIDTH), 1)
        lam = _lambda_value(lam_ref)
        acc_sb = accsb_ref[...]
        acc_df = accdf_ref[...] / l_ref[...]
        o_sb = jnp.zeros((DEC_T, SB_WIDTH), F32)
        o_df = jnp.zeros((DEC_T, DF_WIDTH), F32)
        for h in range(N_SB_HEADS):
            in_h = (lane >> 6) == h
            o_sb = jnp.where(in_h, acc_sb[DEC_T * h:DEC_T * (h + 1), :], o_sb)
            r0 = 2 * DEC_T * h
            o_h = acc_df[r0:r0 + DEC_T, :] - lam * acc_df[r0 + DEC_T:r0 + 2 * DEC_T, :]
            o_df = jnp.where(in_h, o_h, o_df)
        y_df = _group_rms_lanes(o_df, g_ref[...], lane, N_DIFF_HEADS) * (1.0 - LAMBDA_INIT)
        o_ref[:, :SB_WIDTH] = o_sb.astype(BF16)
        o_ref[:, SB_WIDTH:] = y_df.astype(BF16)


def _dec_attention(page_table, q, k_new, v_new, cache_k, cache_v, trio, slope_rep, lam_vecs, g8):
    n_seq, n_pages = page_table.shape
    past_len = n_pages * PAGE_SIZE
    n_steps = n_pages // PAGES_PER_STEP

    def page_spec(r):
        def idx(b, s, pt):
            return (pt[b, n_pages - 1 - (s * PAGES_PER_STEP + r)], 0, 0)
        return pl.BlockSpec((None, D_MODEL, PAGE_SIZE), idx)

    seq_spec = pl.BlockSpec((None, DEC_T, D_MODEL), lambda b, s, pt: (b, 0, 0))
    const2 = lambda b, s, pt: (0, 0)
    grid_spec = pltpu.PrefetchScalarGridSpec(
        num_scalar_prefetch=1,
        grid=(n_seq, n_steps),
        in_specs=[seq_spec, seq_spec, seq_spec]
        + [page_spec(r) for r in range(PAGES_PER_STEP)]
        + [page_spec(r) for r in range(PAGES_PER_STEP)]
        + [
            pl.BlockSpec((PAGE_SIZE, 2 * PAGE_SIZE), const2),
            pl.BlockSpec((DF_ROWS, PAGE_SIZE), const2),
            pl.BlockSpec((4, DIFF_QK_DIM), const2),
            pl.BlockSpec((1, DF_WIDTH), const2),
        ],
        out_specs=seq_spec,
        scratch_shapes=[
            pltpu.VMEM((SB_ROWS, SB_WIDTH), BF16),
            pltpu.VMEM((DF_ROWS, DF_WIDTH), BF16),
            pltpu.VMEM((SB_ROWS, PAGE_SIZE), F32),
            pltpu.VMEM((SB_ROWS, SB_WIDTH), F32),
            pltpu.VMEM((DF_ROWS, 1), F32),
            pltpu.VMEM((DF_ROWS, PAGE_SIZE), F32),
            pltpu.VMEM((DF_ROWS, 1), F32),
            pltpu.VMEM((DF_ROWS, DF_WIDTH), F32),
        ],
    )
    return pl.pallas_call(
        functools.partial(_dec_attn_kernel, past_len=past_len),
        out_shape=jax.ShapeDtypeStruct((n_seq, DEC_T, D_MODEL), BF16),
        grid_spec=grid_spec,
        compiler_params=pltpu.CompilerParams(
            dimension_semantics=("arbitrary", "arbitrary"), vmem_limit_bytes=VMEM_LIMIT),
        name="dec_attn",
    )(page_table, q, k_new, v_new, *([cache_k] * PAGES_PER_STEP), *([cache_v] * PAGES_PER_STEP),
      trio, slope_rep, lam_vecs, g8)


def _attn_out_kernel(ma_ref, mb_ref, x_ref, wa_ref, wb_ref, gpost_ref, gt_ref,
                     gpre_ref, sc_ref, sh_ref, x1_ref, h_ref):
    y = _dot(ma_ref[...], wa_ref[...]) + _dot(mb_ref[...], wb_ref[...])
    x1 = x_ref[...] + gt_ref[...] * _rms(y, gpost_ref[...], NORM_EPS)
    x1_ref[...] = x1
    h = _rms(x1, gpre_ref[...], NORM_EPS) * (1.0 + sc_ref[...]) + sh_ref[...]
    h_ref[...] = h.astype(BF16)


def _attn_out(mixed_a, a_col, mixed_b, b_col, x2d, w_out_bf16, g_post, gt, g_pre, sc, sh,
              tm, tiles_per_group):
    n = x2d.shape[0]
    half = D_MODEL // 2
    row_spec = pl.BlockSpec((tm, D_MODEL), lambda i: (i, 0))
    const2 = lambda i: (0, 0)
    vec_spec = pl.BlockSpec((1, D_MODEL), const2)
    mod = lambda a: _mod_spec(a.shape[1], tiles_per_group)
    return pl.pallas_call(
        _attn_out_kernel,
        out_shape=(jax.ShapeDtypeStruct((n, D_MODEL), F32),
                   jax.ShapeDtypeStruct((n, D_MODEL), BF16)),
        grid=(n // tm,),
        in_specs=[
            pl.BlockSpec((tm, half), lambda i: (i, a_col)),
            pl.BlockSpec((tm, half), lambda i: (i, b_col)),
            row_spec,
            pl.BlockSpec((half, D_MODEL), lambda i: (0, 0)),
            pl.BlockSpec((half, D_MODEL), lambda i: (1, 0)),
            vec_spec, mod(gt), vec_spec, mod(sc), mod(sh),
        ],
        out_specs=(row_spec, row_spec),
        compiler_params=pltpu.CompilerParams(
            dimension_semantics=("arbitrary",), vmem_limit_bytes=VMEM_LIMIT),
        name="attn_out",
    )(mixed_a, mixed_b, x2d, w_out_bf16, w_out_bf16, g_post.reshape(1, D_MODEL), gt,
      g_pre.reshape(1, D_MODEL), sc, sh)


UP_CHUNK = 1408


def _up_conv_kernel(h_ref, prev_ref, wup_ref, cw_ref, cb_ref, g_ref, st_ref, carry_ref, *, tm):
    t = pl.program_id(1)

    @pl.when(t == 0)
    def _load_state():
        carry_ref[...] = prev_ref[...]

    h = h_ref[...]
    row = lax.broadcasted_iota(jnp.int32, (SUBLANES, 1), 0)

    def shifted(u, top):
        return top if tm == SUBLANES else jnp.concatenate([top, u[SUBLANES:]], axis=0)

    def conv_cols(c0):
        u = _dot(h, wup_ref[:, c0:c0 + UP_CHUNK])
        p0 = carry_ref[0:1, c0:c0 + UP_CHUNK]
        p1 = carry_ref[1:2, c0:c0 + UP_CHUNK]
        r1 = pltpu.roll(u, 1, 0)
        r2 = pltpu.roll(u, 2, 0)
        u1 = shifted(r1, jnp.where(row == 0, p1, r1[:SUBLANES]))
        u2 = shifted(r2, jnp.where(row == 0, p0, jnp.where(row == 1, p1, r2[:SUBLANES])))
        w0 = cw_ref[0:1, c0:c0 + UP_CHUNK]
        w1 = cw_ref[1:2, c0:c0 + UP_CHUNK]
        w2 = cw_ref[2:3, c0:c0 + UP_CHUNK]
        conv = cb_ref[:, c0:c0 + UP_CHUNK] + w0 * u2 + w1 * u1 + w2 * u
        last = u[tm - 2:tm, :]
        carry_ref[:, c0:c0 + UP_CHUNK] = last
        st_ref[:, c0:c0 + UP_CHUNK] = last
        return conv

    for ch in range(D_FF // UP_CHUNK):
        a = conv_cols(ch * UP_CHUNK)
        b = conv_cols(D_FF + ch * UP_CHUNK)
        gate = a * (1.0 / (1.0 + jnp.exp(-a))) * b
        g_ref[:, ch * UP_CHUNK:(ch + 1) * UP_CHUNK] = gate.astype(BF16)


def _up_conv(h3d, conv_prev, w_up_bf16, conv_w, conv_b, tm):
    nb, t_len, _ = h3d.shape
    const2 = lambda b, t: (0, 0)
    return pl.pallas_call(
        functools.partial(_up_conv_kernel, tm=tm),
        out_shape=(jax.ShapeDtypeStruct((nb, t_len, D_FF), BF16),
                   jax.ShapeDtypeStruct((nb, CONV_WIDTH - 1, 2 * D_FF), F32)),
        grid=(nb, t_len // tm),
        in_specs=[
            pl.BlockSpec((None, tm, D_MODEL), lambda b, t: (b, t, 0)),
            pl.BlockSpec((None, CONV_WIDTH - 1, 2 * D_FF), lambda b, t: (b, 0, 0)),
            pl.BlockSpec((D_MODEL, 2 * D_FF), const2),
            pl.BlockSpec((CONV_WIDTH, 2 * D_FF), const2),
            pl.BlockSpec((1, 2 * D_FF), const2),
        ],
        out_specs=(pl.BlockSpec((None, tm, D_FF), lambda b, t: (b, t, 0)),
                   pl.BlockSpec((None, CONV_WIDTH - 1, 2 * D_FF), lambda b, t: (b, 0, 0))),
        scratch_shapes=[pltpu.VMEM((CONV_WIDTH - 1, 2 * D_FF), F32)],
        compiler_params=pltpu.CompilerParams(
            dimension_semantics=("arbitrary", "arbitrary"), vmem_limit_bytes=VMEM_LIMIT),
        name="up_conv_gate",
    )(h3d, conv_prev, w_up_bf16, conv_w, conv_b.reshape(1, 2 * D_FF))


def _down_kernel(g_ref, x1_ref, wd_ref, gpost_ref, gt_ref, y_ref):
    f = _dot(g_ref[...], wd_ref[...])
    y_ref[...] = x1_ref[...] + gt_ref[...] * _rms(f, gpost_ref[...], NORM_EPS)


def _down_proj(g2d, x1, w_down_bf16, g_post, gt, tm, tiles_per_group):
    n = x1.shape[0]
    row_spec = pl.BlockSpec((tm, D_MODEL), lambda i: (i, 0))
    const2 = lambda i: (0, 0)
    return pl.pallas_call(
        _down_kernel,
        out_shape=jax.ShapeDtypeStruct((n, D_MODEL), F32),
        grid=(n // tm,),
        in_specs=[
            pl.BlockSpec((tm, D_FF), lambda i: (i, 0)),
            row_spec,
            pl.BlockSpec((D_FF, D_MODEL), const2),
            pl.BlockSpec((1, D_MODEL), const2),
            _mod_spec(gt.shape[1], tiles_per_group),
        ],
        out_specs=row_spec,
        compiler_params=pltpu.CompilerParams(
            dimension_semantics=("arbitrary",), vmem_limit_bytes=VMEM_LIMIT),
        name="down_proj",
    )(g2d, x1, w_down_bf16, g_post.reshape(1, D_MODEL), gt)


def _later_mask(n):
    idx = np.arange(n)
    return (idx[None, :] > idx[:, None]).astype(np.float32)


def _alibi_slopes_np():
    return (2.0 ** (-(8.0 / N_DIFF_HEADS) * np.arange(1, N_DIFF_HEADS + 1))).astype(np.float32)


def _alibi_features(slopes2):
    bf = ml_dtypes.bfloat16
    hi = slopes2.astype(bf).astype(np.float32)
    mid = (slopes2 - hi).astype(bf).astype(np.float32)
    lo = (slopes2 - hi - mid).astype(bf).astype(np.float32)
    sfeat = np.zeros((N_DIFF_HEADS, LANES, TQ), np.float32)
    for row, piece in enumerate((hi, mid, lo)):
        sfeat[:, row, :] = piece[:, None]
    kfeat = np.zeros((TQ, LANES), np.float32)
    kfeat[:, :3] = np.arange(TQ, dtype=np.float32)[:, None]
    return (jnp.asarray(sfeat.reshape(N_DIFF_HEADS // 2, 2, LANES, TQ), dtype=BF16),
            jnp.asarray(kfeat, dtype=BF16))


def kernel(x_prompt, x_sample, c_prompt, c_sample, cache_k, cache_v, state_conv, page_table,
           w_ada, b_ada, g_pre_attn, g_post_attn, w_in, w_out, lambda_q1, lambda_k1,
           lambda_q2, lambda_k2, g_subln, g_pre_mlp, g_post_mlp, w_up, conv_w, conv_b, w_down):
    layer = 0
    n_b, seq, _ = x_prompt.shape
    n_dec, dec_t, _ = x_sample.shape
    assert dec_t == DEC_T
    n_pool = cache_k.shape[1]

    w_in_b = w_in[layer].astype(BF16)
    w_out_b = w_out[layer].astype(BF16)
    w_up_b = w_up[layer].astype(BF16)
    w_down_b = w_down[layer].astype(BF16)

    c_all = jnp.concatenate([c_prompt, c_sample], axis=0)
    mod = _modulation(c_all, w_ada[layer], b_ada[layer])
    mod_p = [m.reshape(n_b, 1, D_MODEL) for m in jnp.split(mod[:n_b], N_MOD, axis=-1)]
    mod_s = [jnp.repeat(m, DEC_T, axis=0).reshape(1, n_dec * DEC_T, D_MODEL)
             for m in jnp.split(mod[n_b:], N_MOD, axis=-1)]

    slopes2 = _alibi_slopes_np() * np.float32(LOG2E)
    slope_pairs = jnp.asarray(np.repeat(slopes2, HEAD_DIM).reshape(N_DIFF_HEADS // 2, 1, LANES))
    slope_rep = jnp.asarray(np.broadcast_to(np.repeat(slopes2, 2 * DEC_T)[:, None],
                                            (DF_ROWS, PAGE_SIZE)))
    sfeat, kfeat = _alibi_features(slopes2)
    lam_vecs = jnp.stack([lambda_q1[layer], lambda_k1[layer],
                          lambda_q2[layer], lambda_k2[layer]]).astype(F32)
    g_sub = g_subln[layer].astype(F32)
    g_col = jnp.tile(g_sub, 2).reshape(LANES, 1)
    g8 = jnp.tile(g_sub, N_DIFF_HEADS).reshape(1, DF_WIDTH)
    usuf = jnp.asarray(_later_mask(TQ), dtype=BF16)
    tri = _later_mask(PAGE_SIZE).T
    trio = jnp.asarray(np.concatenate([tri, np.ones_like(tri)], axis=1), dtype=BF16)

    tm_p = TQ
    tiles_p = seq // tm_p
    xp2d = x_prompt.reshape(n_b * seq, D_MODEL)
    sh_a, sc_a, gt_a, sh_m, sc_m, gt_m = mod_p
    qt_p, kt_p, vt_p, kb_p, vtb_p = _qkv_prompt(xp2d, sc_a, sh_a, g_pre_attn[layer], w_in_b,
                                                n_b, seq, tm_p)
    kb4 = kb_p.reshape(n_b, tiles_p, TQ, D_MODEL)
    mixed_sb = _sb_prompt_attention(qt_p, kb4, vtb_p, usuf)
    mixed_df = _diff_prompt_attention(qt_p, kb4, vtb_p, slope_pairs, sfeat, kfeat, lam_vecs, g_col)
    x1_p, h_p = _attn_out(mixed_sb.reshape(n_b * seq, SB_WIDTH), 0,
                          mixed_df.reshape(n_b * seq, DF_WIDTH), 0,
                          xp2d, w_out_b, g_post_attn[layer], gt_a, g_pre_mlp[layer], sc_m, sh_m,
                          tm_p, tiles_p)
    conv0 = jnp.zeros((n_b, CONV_WIDTH - 1, 2 * D_FF), F32)
    gate_p, conv_p = _up_conv(h_p.reshape(n_b, seq, D_MODEL), conv0, w_up_b,
                              conv_w[layer], conv_b[layer], tm_p)
    y_p = _down_proj(gate_p.reshape(n_b * seq, D_FF), x1_p, w_down_b, g_post_mlp[layer],
                     gt_m, tm_p, tiles_p)

    tm_s = n_dec * DEC_T
    xs2d = x_sample.reshape(tm_s, D_MODEL)
    sh_a, sc_a, gt_a, sh_m, sc_m, gt_m = mod_s
    q_s, k_s, v_s = _qkv_rows(xs2d, sc_a, sh_a, g_pre_attn[layer], w_in_b, tm_s)
    per_seq = lambda a: a.reshape(n_dec, DEC_T, D_MODEL)
    pages = lambda c: c[layer].transpose(0, 2, 3, 1).reshape(n_pool, D_MODEL, PAGE_SIZE)
    mixed_s = _dec_attention(
        page_table, per_seq(q_s), per_seq(k_s), per_seq(v_s), pages(cache_k), pages(cache_v),
        trio, slope_rep, lam_vecs, g8)
    mixed_s2d = mixed_s.reshape(tm_s, D_MODEL)
    x1_s, h_s = _attn_out(mixed_s2d, 0, mixed_s2d, 1, xs2d, w_out_b, g_post_attn[layer], gt_a,
                          g_pre_mlp[layer], sc_m, sh_m, tm_s, 1)
    gate_s, conv_s = _up_conv(h_s.reshape(n_dec, DEC_T, D_MODEL), state_conv[layer].astype(F32),
                              w_up_b, conv_w[layer], conv_b[layer], DEC_T)
    y_s = _down_proj(gate_s.reshape(tm_s, D_FF), x1_s, w_down_b, g_post_mlp[layer], gt_m, tm_s, 1)

    heads = lambda a, b, t: a.reshape(1, b, t, N_HEADS, HEAD_DIM)
    heads_t = lambda a: a.reshape(1, n_b, N_HEADS, HEAD_DIM, seq).transpose(0, 1, 4, 2, 3)
    return (y_p.reshape(n_b, seq, D_MODEL),
            y_s.reshape(n_dec, DEC_T, D_MODEL),
            heads_t(kt_p), heads_t(vt_p), conv_p[None],
            heads(k_s, n_dec, DEC_T), heads(v_s, n_dec, DEC_T), conv_s[None])
```

```python
import functools
import math

import jax
import jax.numpy as jnp
import ml_dtypes
import numpy as np
from jax import lax
from jax.experimental import pallas as pl
from jax.experimental.pallas import tpu as pltpu

F32 = jnp.float32
BF16 = jnp.bfloat16

D_MODEL = 1024
HEAD_DIM = 64
N_SB_HEADS = 8
N_DIFF_HEADS = 8
N_HEADS = N_SB_HEADS + N_DIFF_HEADS
SB_WIDTH = N_SB_HEADS * HEAD_DIM
DF_WIDTH = N_DIFF_HEADS * HEAD_DIM
DIFF_QK_DIM = HEAD_DIM // 2
D_FF = 2816
CONV_WIDTH = 3
PAGE_SIZE = 128
NORM_EPS = 1e-6
SUBLN_EPS = 1e-5
N_MOD = 6
LAMBDA_INIT = 0.8 - 0.6 * math.exp(-0.3 * 0)

LANES = 128
NEG_BIG = -1e30
LOG2E = math.log2(math.e)
DEAD_LOG2 = 104.0 * LOG2E + 1.0
BOUND_SLACK = 1.001

VMEM_LIMIT = 56 * 1024 * 1024


def _dot(a, b):
    return jnp.dot(a, b, preferred_element_type=F32)


def _dot_nt(a, b):
    return lax.dot_general(a, b, (((1,), (1,)), ((), ())), preferred_element_type=F32)


def _rms(x, g, eps):
    return x * lax.rsqrt(jnp.mean(x * x, axis=-1, keepdims=True) + eps) * g


def _softplus2(z):
    return jnp.maximum(z, 0.0) + jnp.log2(1.0 + jnp.exp2(-jnp.abs(z)))


def _split_bf16(x):
    hi = x.astype(BF16)
    lo = (x - hi.astype(F32)).astype(BF16)
    return hi, lo


def _mod_kernel(c_ref, w_ref, b_ref, o_ref):
    c = c_ref[...]
    s = c * (1.0 / (1.0 + jnp.exp(-c)))
    o_ref[...] = _dot(s.astype(BF16), w_ref[...].astype(BF16)) + b_ref[...]


def _modulation(c_all, w_ada, b_ada):
    n_rows = c_all.shape[0]
    n_out = w_ada.shape[1]
    tn = 1536
    return pl.pallas_call(
        _mod_kernel,
        out_shape=jax.ShapeDtypeStruct((n_rows, n_out), F32),
        grid=(n_out // tn,),
        in_specs=[
            pl.BlockSpec((n_rows, D_MODEL), lambda j: (0, 0)),
            pl.BlockSpec((D_MODEL, tn), lambda j: (0, j)),
            pl.BlockSpec((1, tn), lambda j: (0, j)),
        ],
        out_specs=pl.BlockSpec((n_rows, tn), lambda j: (0, j)),
        compiler_params=pltpu.CompilerParams(
            dimension_semantics=("arbitrary",), vmem_limit_bytes=VMEM_LIMIT),
        name="adaln_mod",
    )(c_all, w_ada, b_ada.reshape(1, n_out))


def _scaled_qkv(x_ref, sc_ref, sh_ref, g_ref, w_ref):
    x = x_ref[...]
    h = _rms(x, g_ref[...], NORM_EPS) * (1.0 + sc_ref[...]) + sh_ref[...]
    qkv = _dot(h.astype(BF16), w_ref[...])
    lane = lax.broadcasted_iota(jnp.int32, (1, D_MODEL), 1)
    qscale = jnp.where(lane < SB_WIDTH, LOG2E * HEAD_DIM ** -0.5,
                       LOG2E * DIFF_QK_DIM ** -0.5).astype(F32)
    return qkv[:, :D_MODEL] * qscale, qkv[:, D_MODEL:2 * D_MODEL], qkv[:, 2 * D_MODEL:]


def _qkv_rows_kernel(x_ref, sc_ref, sh_ref, g_ref, w_ref, q_ref, k_ref, v_ref):
    q, k, v = _scaled_qkv(x_ref, sc_ref, sh_ref, g_ref, w_ref)
    q_ref[...] = q.astype(BF16)
    k_ref[...] = k
    v_ref[...] = v


def _qkv_prompt_kernel(x_ref, sc_ref, sh_ref, g_ref, w_ref, ind_ref,
                       qt_ref, kt_ref, vt_ref, kb_ref, vtb_ref, kn_ref):
    q, k, v = _scaled_qkv(x_ref, sc_ref, sh_ref, g_ref, w_ref)
    qt_ref[...] = q.T.astype(BF16)
    kt_ref[...] = k.T
    vt = v.T
    vt_ref[...] = vt
    vtb_ref[...] = vt.astype(BF16)
    kb = k.astype(BF16)
    kb_ref[...] = kb
    kf = kb.astype(F32)
    hi, lo = _split_bf16(kf * kf)
    ind = ind_ref[...]
    norms2 = _dot(hi, ind) + _dot(lo, ind)
    kn_ref[...] = jnp.max(norms2, axis=0, keepdims=True) * (1.0 + 2.0 ** -12)


def _mod_spec(mod_rows, tiles_per_group):
    return pl.BlockSpec((None, mod_rows, D_MODEL), lambda i: (i // tiles_per_group, 0, 0))


def _qkv_in_specs(sc, sh, tm, tiles_per_group):
    const2 = lambda i: (0, 0)
    return [
        pl.BlockSpec((tm, D_MODEL), lambda i: (i, 0)),
        _mod_spec(sc.shape[1], tiles_per_group),
        _mod_spec(sh.shape[1], tiles_per_group),
        pl.BlockSpec((1, D_MODEL), const2),
        pl.BlockSpec((D_MODEL, 3 * D_MODEL), const2),
    ]


def _qkv_rows(x2d, sc, sh, g, w_bf16, tm):
    n = x2d.shape[0]
    row_spec = pl.BlockSpec((tm, D_MODEL), lambda i: (i, 0))
    out_f32 = jax.ShapeDtypeStruct((n, D_MODEL), F32)
    return pl.pallas_call(
        _qkv_rows_kernel,
        out_shape=(jax.ShapeDtypeStruct((n, D_MODEL), BF16), out_f32, out_f32),
        grid=(n // tm,),
        in_specs=_qkv_in_specs(sc, sh, tm, 1),
        out_specs=(row_spec, row_spec, row_spec),
        compiler_params=pltpu.CompilerParams(
            dimension_semantics=("arbitrary",), vmem_limit_bytes=VMEM_LIMIT),
        name="qkv_rows",
    )(x2d, sc, sh, g.reshape(1, D_MODEL), w_bf16)


def _qkv_prompt(x2d, sc, sh, g, w_bf16, n_b, seq, tm):
    tiles = seq // tm
    t_spec = pl.BlockSpec((None, D_MODEL, tm), lambda i: (i // tiles, 0, i % tiles))
    t_f32 = jax.ShapeDtypeStruct((n_b, D_MODEL, seq), F32)
    group_of_lane = np.arange(D_MODEL)[:, None] // DIFF_QK_DIM == np.arange(LANES)[None, :]
    ind = jnp.asarray(group_of_lane.astype(np.float32), dtype=BF16)
    return pl.pallas_call(
        _qkv_prompt_kernel,
        out_shape=(jax.ShapeDtypeStruct((n_b, D_MODEL, seq), BF16), t_f32, t_f32,
                   jax.ShapeDtypeStruct((n_b * seq, D_MODEL), BF16),
                   jax.ShapeDtypeStruct((n_b, tiles, D_MODEL, tm), BF16),
                   jax.ShapeDtypeStruct((n_b * tiles, 1, LANES), F32)),
        grid=(n_b * tiles,),
        in_specs=_qkv_in_specs(sc, sh, tm, tiles) + [pl.BlockSpec((D_MODEL, LANES), lambda i: (0, 0))],
        out_specs=(t_spec, t_spec, t_spec,
                   pl.BlockSpec((tm, D_MODEL), lambda i: (i, 0)),
                   pl.BlockSpec((None, None, D_MODEL, tm), lambda i: (i // tiles, i % tiles, 0, 0)),
                   pl.BlockSpec((None, 1, LANES), lambda i: (i, 0, 0))),
        compiler_params=pltpu.CompilerParams(
            dimension_semantics=("arbitrary",), vmem_limit_bytes=VMEM_LIMIT),
        name="qkv_prompt",
    )(x2d, sc, sh, g.reshape(1, D_MODEL), w_bf16, ind)


def _prefix_key_norms(kn2, n_b, tiles):
    n_groups = 2 * N_DIFF_HEADS
    first = SB_WIDTH // DIFF_QK_DIM
    kn = jnp.sqrt(kn2.reshape(n_b, tiles, LANES)[:, :, first:first + n_groups])
    kn = lax.cummax(kn, axis=1)
    kn = kn.reshape(n_b, tiles, N_DIFF_HEADS // 2, 4).transpose(0, 2, 1, 3)
    return jnp.pad(kn, ((0, 0), (0, 0), (0, 0), (0, LANES - 4)))


def _lambda_value(lam_ref):
    lq1 = lam_ref[0:1, :]
    lk1 = lam_ref[1:2, :]
    lq2 = lam_ref[2:3, :]
    lk2 = lam_ref[3:4, :]
    return (jnp.exp(jnp.sum(lq1 * lk1, axis=1, keepdims=True))
            - jnp.exp(jnp.sum(lq2 * lk2, axis=1, keepdims=True)) + LAMBDA_INIT)


TQ = 256
DIFF_TRIPS = ((1, 1), (2, 1), (4, 1), (8, 1 << 30), (4, 1), (2, 1), (1, 1))


def _prompt_specs(s, first_pair):
    n_blk = s // TQ
    return [
        pl.BlockSpec((None, LANES, TQ), lambda bi, p, i: (bi, first_pair + p, i)),
        pl.BlockSpec((None, n_blk, TQ, LANES), lambda bi, p, i: (bi, 0, 0, first_pair + p)),
        pl.BlockSpec((None, n_blk, LANES, TQ), lambda bi, p, i: (bi, 0, first_pair + p, 0)),
    ]


def _sb_prompt_kernel(qt_ref, k_ref, vt_ref, usuf_ref, o_ref, acc_ref):
    i = pl.program_id(2)
    qt = qt_ref[...].astype(F32)
    drow = lax.broadcasted_iota(jnp.int32, (LANES, TQ), 0)
    key = lax.broadcasted_iota(jnp.int32, (TQ, TQ), 0)
    qry = lax.broadcasted_iota(jnp.int32, (TQ, TQ), 1)
    strictly_causal = key < qry
    usuf = usuf_ref[...]
    in_head = [(drow >= HEAD_DIM * h) & (drow < HEAD_DIM * (h + 1)) for h in range(2)]
    qtm = [jnp.where(in_head[h], qt, 0.0).astype(BF16) for h in range(2)]

    def step(js, runs, masked_first):
        streams = [(b, h) for b in range(len(js)) for h in range(2)]
        kbs = [k_ref[j] for j in js]
        vts = [vt_ref[j] for j in js]
        zs = {(b, h): _dot(kbs[b], qtm[h]) for b, h in streams}
        log_sig, parts, offsets = {}, {}, {}
        runs = list(runs)
        for b, h in streams:
            masked = masked_first and b == 0
            sp = _softplus2(zs[b, h])
            log_sig[b, h] = zs[b, h] - sp
            if masked:
                sp = jnp.where(strictly_causal, sp, 0.0)
            parts[b, h] = _split_bf16(sp)
            offsets[b, h] = runs[h]
            runs[h] = runs[h] + jnp.sum(sp, axis=0, keepdims=True)
        between = {s: _dot(usuf, parts[s][0]) + _dot(usuf, parts[s][1]) for s in streams}
        weights = {}
        for b, h in streams:
            w = jnp.exp2(log_sig[b, h] - between[b, h] - offsets[b, h])
            if masked_first and b == 0:
                w = jnp.where(strictly_causal, w, 0.0)
            weights[b, h] = w.astype(BF16)
        outs = {s: _dot(vts[s[0]], weights[s]) for s in streams}
        for h in range(2):
            acc = acc_ref[h]
            for b in range(len(js)):
                acc = acc + outs[b, h]
            acc_ref[h] = acc
        return tuple(runs)

    def alive(runs):
        return (jnp.min(jnp.minimum(runs[0], runs[1])) < DEAD_LOG2).astype(jnp.int32)

    acc_ref[...] = jnp.zeros_like(acc_ref)
    zero = jnp.zeros((1, TQ), F32)
    runs = lax.cond(i > 0,
                    lambda: step([i, i - 1], (zero, zero), True),
                    lambda: step([i], (zero, zero), True))

    def cond(c):
        return (c[0] < i) & (c[1] > 0)

    def body(c):
        runs = step([i - 1 - c[0]], (c[2], c[3]), False)
        return c[0] + 1, alive(runs), runs[0], runs[1]

    lax.while_loop(cond, body, (jnp.int32(1), alive(runs), runs[0], runs[1]))
    out_t = jnp.where(in_head[0], acc_ref[0], acc_ref[1])
    o_ref[...] = out_t.T.astype(BF16)


def _sb_prompt_attention(qt, kb4, vtb4, usuf):
    b, _, s = qt.shape
    n_pairs = SB_WIDTH // LANES
    return pl.pallas_call(
        _sb_prompt_kernel,
        out_shape=jax.ShapeDtypeStruct((b, s, SB_WIDTH), BF16),
        grid=(b, n_pairs, s // TQ),
        in_specs=_prompt_specs(s, 0) + [pl.BlockSpec((TQ, TQ), lambda bi, p, i: (0, 0))],
        out_specs=pl.BlockSpec((None, TQ, LANES), lambda bi, p, i: (bi, i, p)),
        scratch_shapes=[pltpu.VMEM((2, LANES, TQ), F32)],
        compiler_params=pltpu.CompilerParams(
            dimension_semantics=("arbitrary", "arbitrary", "arbitrary"),
            vmem_limit_bytes=VMEM_LIMIT),
        name="sb_prompt_attn",
    )(qt, kb4, vtb4, usuf)


def _diff_prompt_kernel(qt_ref, k_ref, vt_ref, slope_ref, sfeat_ref, kfeat_ref, kmax_ref, lam_ref,
                        g_ref, o_ref, acc_ref):
    i = pl.program_id(2)
    qt = qt_ref[...].astype(F32)
    drow = lax.broadcasted_iota(jnp.int32, (LANES, TQ), 0)
    key = lax.broadcasted_iota(jnp.int32, (TQ, TQ), 0)
    qry = lax.broadcasted_iota(jnp.int32, (TQ, TQ), 1)
    causal = key <= qry
    lam = _lambda_value(lam_ref)
    slopes = [slope_ref[:, HEAD_DIM * h:HEAD_DIM * h + 1] for h in range(2)]
    kfeat = kfeat_ref[...]
    qext = []
    for h in range(2):
        for c in range(2):
            lo_row = HEAD_DIM * h + DIFF_QK_DIM * c
            in_map = (drow >= lo_row) & (drow < lo_row + DIFF_QK_DIM)
            qtm = jnp.where(in_map, qt, 0.0).astype(BF16)
            qext.append(jnp.concatenate([qtm, sfeat_ref[h]], axis=0))

    head_rows = [(drow >= HEAD_DIM * h) & (drow < HEAD_DIM * (h + 1)) for h in range(2)]
    keep = [jnp.where(head_rows[h], 1.0, 0.0).astype(BF16) for h in range(2)]
    fill = [jnp.where(head_rows[h], 0.0, 1.0).astype(BF16) for h in range(2)]

    def step(js, ms, masked):
        ms = list(ms)
        kexts = [jnp.concatenate([k_ref[j], kfeat], axis=1) for j in js]
        scores = [[_dot(kext, qext[idx]) for idx in range(4)] for kext in kexts]
        pending = []
        for b, j in enumerate(js):
            alphas, probs = [], []
            for idx in range(4):
                h = idx // 2
                s = scores[b][idx]
                if masked:
                    s = jnp.where(causal, s, NEG_BIG)
                shift = slopes[h] * ((i - j) * TQ).astype(F32)
                m_new = jnp.maximum(ms[idx], jnp.max(s, axis=0, keepdims=True) - shift)
                alphas.append(jnp.exp2(ms[idx] - m_new))
                probs.append(jnp.exp2(s - (m_new + shift)).astype(BF16))
                ms[idx] = m_new
            vt = vt_ref[j]
            vts = [vt * keep[h] + fill[h] for h in range(2)]
            pending.append((alphas, [_dot(vts[idx // 2], probs[idx]) for idx in range(4)]))
        for idx in range(4):
            acc = acc_ref[idx]
            for alphas, outs in pending:
                acc = alphas[idx] * acc + outs[idx]
            acc_ref[idx] = acc
        return tuple(ms)

    q_norm = []
    for idx in range(4):
        lo_row = DIFF_QK_DIM * idx
        in_map = (drow >= lo_row) & (drow < lo_row + DIFF_QK_DIM)
        q_norm.append(jnp.sqrt(jnp.sum(jnp.where(in_map, qt * qt, 0.0), axis=0, keepdims=True)))

    def alive(rem, ms):
        k_norm = kmax_ref[pl.ds(jnp.maximum(rem - 1, 0), 1), :]
        gap = ((TQ - 1) - (i - rem + 1) * TQ).astype(F32)
        worst = jnp.full((1, TQ), NEG_BIG, F32)
        for idx in range(4):
            bound = q_norm[idx] * k_norm[:, idx:idx + 1] * BOUND_SLACK + slopes[idx // 2] * gap
            worst = jnp.maximum(worst, bound - ms[idx])
        return (jnp.max(worst) > -DEAD_LOG2).astype(jnp.int32)

    acc_ref[...] = jnp.zeros_like(acc_ref)
    ms = step([i], (jnp.full((1, TQ), NEG_BIG, F32),) * 4, True)

    state = (i, alive(i, ms)) + ms
    for group, max_trips in DIFF_TRIPS:
        def cond(c, group=group, max_trips=max_trips):
            return (c[0] < max_trips) & (c[1] >= group) & (c[2] > 0)

        def body(c, group=group):
            rem, ms = c[1], c[3:]
            still = alive(rem - group, ms)
            ms = step([rem - group + u for u in range(group)], ms, False)
            return (c[0] + 1, rem - group, still) + ms

        state = lax.while_loop(cond, body, (jnp.int32(0),) + state)[1:]

    head0 = drow < HEAD_DIM
    o_t = []
    for h in range(2):
        row = HEAD_DIM * (1 - h)
        maps = [acc_ref[2 * h + c] / acc_ref[2 * h + c, row:row + 1, :] for c in range(2)]
        o_t.append(maps[0] - lam * maps[1])
    out_t = jnp.where(head0, o_t[0], o_t[1])
    sq = out_t * out_t
    ss0 = jnp.sum(jnp.where(head0, sq, 0.0), axis=0, keepdims=True)
    ss1 = jnp.sum(jnp.where(head0, 0.0, sq), axis=0, keepdims=True)
    ms_t = jnp.where(head0, ss0, ss1) * (1.0 / HEAD_DIM)
    y_t = out_t * lax.rsqrt(ms_t + SUBLN_EPS) * g_ref[...] * (1.0 - LAMBDA_INIT)
    o_ref[...] = y_t.T.astype(BF16)


def _diff_prompt_attention(qt, kb4, vtb4, slopes, sfeat, kfeat, kmax, lam_vecs, g_col):
    b, _, s = qt.shape
    n_pairs = DF_WIDTH // LANES
    const2 = lambda bi, p, i: (0, 0)
    return pl.pallas_call(
        _diff_prompt_kernel,
        out_shape=jax.ShapeDtypeStruct((b, s, DF_WIDTH), BF16),
        grid=(b, n_pairs, s // TQ),
        in_specs=_prompt_specs(s, SB_WIDTH // LANES) + [
            pl.BlockSpec((None, 1, LANES), lambda bi, p, i: (p, 0, 0)),
            pl.BlockSpec((None, 2, LANES, TQ), lambda bi, p, i: (p, 0, 0, 0)),
            pl.BlockSpec((TQ, LANES), const2),
            pl.BlockSpec((None, None, s // TQ, LANES), lambda bi, p, i: (bi, p, 0, 0)),
            pl.BlockSpec((4, DIFF_QK_DIM), const2),
            pl.BlockSpec((LANES, 1), const2),
        ],
        out_specs=pl.BlockSpec((None, TQ, LANES), lambda bi, p, i: (bi, i, p)),
        scratch_shapes=[pltpu.VMEM((4, LANES, TQ), F32)],
        compiler_params=pltpu.CompilerParams(
            dimension_semantics=("arbitrary", "arbitrary", "arbitrary"),
            vmem_limit_bytes=VMEM_LIMIT),
        name="diff_prompt_attn",
    )(qt, kb4, vtb4, slopes, sfeat, kfeat, kmax, lam_vecs, g_col)


PAGES_PER_STEP = 8
DEC_T = 8
SB_ROWS = N_SB_HEADS * DEC_T
DF_ROWS = N_DIFF_HEADS * 2 * DEC_T


def _group_rms_lanes(o, g, lane, n_groups):
    sq = o * o
    ms = jnp.zeros_like(o)
    for h in range(n_groups):
        in_h = (lane >= HEAD_DIM * h) & (lane < HEAD_DIM * (h + 1))
        s_h = jnp.sum(jnp.where(in_h, sq, 0.0), axis=1, keepdims=True)
        ms = jnp.where(in_h, s_h * (1.0 / HEAD_DIM), ms)
    return o * lax.rsqrt(ms + SUBLN_EPS) * g


def _dec_attn_kernel(pt_ref, q_ref, kn_ref, vn_ref, *rest, past_len):
    del pt_ref
    kp_refs = rest[:PAGES_PER_STEP]
    vp_refs = rest[PAGES_PER_STEP:2 * PAGES_PER_STEP]
    (trio_ref, slope_ref, lam_ref, g_ref, o_ref, wsb_ref, wdf_ref,
     run_ref, accsb_ref, mcol_ref, mrep_ref, l_ref, accdf_ref) = rest[2 * PAGES_PER_STEP:]
    step_id = pl.program_id(1)
    n_steps = pl.num_programs(1)

    key_lane = lax.broadcasted_iota(jnp.int32, (DF_ROWS, PAGE_SIZE), 1)
    t_df = lax.broadcasted_iota(jnp.int32, (DF_ROWS, PAGE_SIZE), 0) & (DEC_T - 1)
    slope = slope_ref[...]
    bias0 = slope * (key_lane - past_len - t_df).astype(F32)
    trio = trio_ref[...]

    def sb_group(chunks):
        offset = run_ref[...]
        acc = accsb_ref[...]
        parts = []
        for z, _, mask in chunks:
            sp = _softplus2(z)
            log_sig = z - sp
            if mask is not None:
                sp = jnp.where(mask, sp, 0.0)
            parts.append((log_sig, _split_bf16(sp)))
        sums = [_dot(hi, trio) + _dot(lo, trio) for _, (hi, lo) in parts]
        for (_, pv, mask), (log_sig, _), sm in zip(chunks, parts, sums):
            w = jnp.exp2(log_sig - sm[:, :PAGE_SIZE] - offset)
            if mask is not None:
                w = jnp.where(mask, w, 0.0)
            acc = acc + pv(w.astype(BF16))
            offset = offset + sm[:, PAGE_SIZE:]
        run_ref[...] = offset
        accsb_ref[...] = acc

    def df_group(chunks):
        top = chunks[0][0]
        for s, _ in chunks[1:]:
            top = jnp.maximum(top, s)
        m_old = mcol_ref[...]
        m_new = jnp.maximum(m_old, jnp.max(top, axis=1, keepdims=True))
        m_rep_old = mrep_ref[...]
        m_rep = jnp.broadcast_to(m_new, (DF_ROWS, PAGE_SIZE))
        p_sum = jnp.zeros((DF_ROWS, PAGE_SIZE), F32)
        pv_sum = jnp.zeros((DF_ROWS, DF_WIDTH), F32)
        for s, pv in chunks:
            p = jnp.exp2(s - m_rep)
            p_sum = p_sum + p
            pv_sum = pv_sum + pv(p.astype(BF16))
        alpha_rep = jnp.exp2(m_rep_old - m_rep)
        l_ref[...] = jnp.exp2(m_old - m_new) * l_ref[...] + jnp.sum(p_sum, axis=1, keepdims=True)
        accdf_ref[...] = jnp.concatenate([alpha_rep] * (DF_WIDTH // PAGE_SIZE), axis=1) * accdf_ref[...] + pv_sum
        mcol_ref[...] = m_new
        mrep_ref[...] = m_rep

    @pl.when(step_id == 0)
    def _init():
        qf = q_ref[...].astype(F32)
        q_sb = jnp.concatenate([qf[:, :SB_WIDTH]] * N_SB_HEADS, axis=0)
        r = lax.broadcasted_iota(jnp.int32, (SB_ROWS, SB_WIDTH), 0)
        c = lax.broadcasted_iota(jnp.int32, (SB_ROWS, SB_WIDTH), 1)
        wsb_ref[...] = jnp.where((r >> 3) == (c >> 6), q_sb, 0.0).astype(BF16)
        q_df = jnp.concatenate([qf[:, SB_WIDTH:]] * (2 * N_DIFF_HEADS), axis=0)
        r = lax.broadcasted_iota(jnp.int32, (DF_ROWS, DF_WIDTH), 0)
        c = lax.broadcasted_iota(jnp.int32, (DF_ROWS, DF_WIDTH), 1)
        wdf_ref[...] = jnp.where((r >> 3) == (c >> 5), q_df, 0.0).astype(BF16)
        run_ref[...] = jnp.zeros_like(run_ref)
        accsb_ref[...] = jnp.zeros_like(accsb_ref)
        mcol_ref[...] = jnp.full_like(mcol_ref, NEG_BIG)
        mrep_ref[...] = jnp.full_like(mrep_ref, NEG_BIG)
        l_ref[...] = jnp.zeros_like(l_ref)
        accdf_ref[...] = jnp.zeros_like(accdf_ref)
        pad = jnp.zeros((PAGE_SIZE - DEC_T, D_MODEL), F32)
        kn = jnp.concatenate([kn_ref[...], pad], axis=0).astype(BF16)
        vn = jnp.concatenate([vn_ref[...], pad], axis=0).astype(BF16)
        key_sb = lax.broadcasted_iota(jnp.int32, (SB_ROWS, PAGE_SIZE), 1)
        t_sb = lax.broadcasted_iota(jnp.int32, (SB_ROWS, PAGE_SIZE), 0) & (DEC_T - 1)
        sb_group([(_dot_nt(wsb_ref[...], kn[:, :SB_WIDTH]),
                   lambda w: _dot(w, vn[:, :SB_WIDTH]), key_sb < t_sb)])
        s_new = _dot_nt(wdf_ref[...], kn[:, SB_WIDTH:]) + (bias0 + slope * float(past_len))
        df_group([(jnp.where(key_lane <= t_df, s_new, NEG_BIG), lambda p: _dot(p, vn[:, SB_WIDTH:]))])

    n_pages = n_steps * PAGES_PER_STEP
    wsb = wsb_ref[...]
    wdf = wdf_ref[...]
    sb_chunks, df_chunks = [], []
    for r in range(PAGES_PER_STEP):
        page = n_pages - 1 - (step_id * PAGES_PER_STEP + r)
        kp = kp_refs[r][...].astype(BF16)
        vp = vp_refs[r][...].astype(BF16)
        sb_chunks.append((_dot(wsb, kp[:SB_WIDTH, :]),
                          functools.partial(_dot_nt, b=vp[:SB_WIDTH, :]), None))
        base = (page * PAGE_SIZE).astype(F32)
        df_chunks.append((_dot(wdf, kp[SB_WIDTH:, :]) + (bias0 + slope * base),
                          functools.partial(_dot_nt, b=vp[SB_WIDTH:, :])))
    sb_group(sb_chunks)
    df_group(df_chunks)

    @pl.when(step_id == n_steps - 1)
    def _finish():
        lane = lax.broadcasted_iota(jnp.int32, (DEC_T, SB_WIDTH), 1)
        lam = _lambda_value(lam_ref)
        acc_sb = accsb_ref[...]
        acc_df = accdf_ref[...] / l_ref[...]
        o_sb = jnp.zeros((DEC_T, SB_WIDTH), F32)
        o_df = jnp.zeros((DEC_T, DF_WIDTH), F32)
        for h in range(N_SB_HEADS):
            in_h = (lane >> 6) == h
            o_sb = jnp.where(in_h, acc_sb[DEC_T * h:DEC_T * (h + 1), :], o_sb)
            r0 = 2 * DEC_T * h
            o_h = acc_df[r0:r0 + DEC_T, :] - lam * acc_df[r0 + DEC_T:r0 + 2 * DEC_T, :]
            o_df = jnp.where(in_h, o_h, o_df)
        y_df = _group_rms_lanes(o_df, g_ref[...], lane, N_DIFF_HEADS) * (1.0 - LAMBDA_INIT)
        o_ref[:, :SB_WIDTH] = o_sb.astype(BF16)
        o_ref[:, SB_WIDTH:] = y_df.astype(BF16)


def _dec_attention(page_table, q, k_new, v_new, cache_k, cache_v, trio, slope_rep, lam_vecs, g8):
    n_seq, n_pages = page_table.shape
    past_len = n_pages * PAGE_SIZE
    n_steps = n_pages // PAGES_PER_STEP

    def page_spec(r):
        def idx(b, s, pt):
            return (pt[b, n_pages - 1 - (s * PAGES_PER_STEP + r)], 0, 0)
        return pl.BlockSpec((None, D_MODEL, PAGE_SIZE), idx)

    seq_spec = pl.BlockSpec((None, DEC_T, D_MODEL), lambda b, s, pt: (b, 0, 0))
    const2 = lambda b, s, pt: (0, 0)
    grid_spec = pltpu.PrefetchScalarGridSpec(
        num_scalar_prefetch=1,
        grid=(n_seq, n_steps),
        in_specs=[seq_spec, seq_spec, seq_spec]
        + [page_spec(r) for r in range(PAGES_PER_STEP)]
        + [page_spec(r) for r in range(PAGES_PER_STEP)]
        + [
            pl.BlockSpec((PAGE_SIZE, 2 * PAGE_SIZE), const2),
            pl.BlockSpec((DF_ROWS, PAGE_SIZE), const2),
            pl.BlockSpec((4, DIFF_QK_DIM), const2),
            pl.BlockSpec((1, DF_WIDTH), const2),
        ],
        out_specs=seq_spec,
        scratch_shapes=[
            pltpu.VMEM((SB_ROWS, SB_WIDTH), BF16),
            pltpu.VMEM((DF_ROWS, DF_WIDTH), BF16),
            pltpu.VMEM((SB_ROWS, PAGE_SIZE), F32),
            pltpu.VMEM((SB_ROWS, SB_WIDTH), F32),
            pltpu.VMEM((DF_ROWS, 1), F32),
            pltpu.VMEM((DF_ROWS, PAGE_SIZE), F32),
            pltpu.VMEM((DF_ROWS, 1), F32),
            pltpu.VMEM((DF_ROWS, DF_WIDTH), F32),
        ],
    )
    return pl.pallas_call(
        functools.partial(_dec_attn_kernel, past_len=past_len),
        out_shape=jax.ShapeDtypeStruct((n_seq, DEC_T, D_MODEL), BF16),
        grid_spec=grid_spec,
        compiler_params=pltpu.CompilerParams(
            dimension_semantics=("arbitrary", "arbitrary"), vmem_limit_bytes=VMEM_LIMIT),
        name="dec_attn",
    )(page_table, q, k_new, v_new, *([cache_k] * PAGES_PER_STEP), *([cache_v] * PAGES_PER_STEP),
      trio, slope_rep, lam_vecs, g8)


def _attn_out_kernel(ma_ref, mb_ref, x_ref, wa_ref, wb_ref, gpost_ref, gt_ref,
                     gpre_ref, sc_ref, sh_ref, x1_ref, h_ref):
    y = _dot(ma_ref[...], wa_ref[...]) + _dot(mb_ref[...], wb_ref[...])
    x1 = x_ref[...] + gt_ref[...] * _rms(y, gpost_ref[...], NORM_EPS)
    x1_ref[...] = x1
    h = _rms(x1, gpre_ref[...], NORM_EPS) * (1.0 + sc_ref[...]) + sh_ref[...]
    h_ref[...] = h.astype(BF16)


def _attn_out(mixed_a, a_col, mixed_b, b_col, x2d, w_out_bf16, g_post, gt, g_pre, sc, sh,
              tm, tiles_per_group):
    n = x2d.shape[0]
    half = D_MODEL // 2
    row_spec = pl.BlockSpec((tm, D_MODEL), lambda i: (i, 0))
    const2 = lambda i: (0, 0)
    vec_spec = pl.BlockSpec((1, D_MODEL), const2)
    mod = lambda a: _mod_spec(a.shape[1], tiles_per_group)
    return pl.pallas_call(
        _attn_out_kernel,
        out_shape=(jax.ShapeDtypeStruct((n, D_MODEL), F32),
                   jax.ShapeDtypeStruct((n, D_MODEL), BF16)),
        grid=(n // tm,),
        in_specs=[
            pl.BlockSpec((tm, half), lambda i: (i, a_col)),
            pl.BlockSpec((tm, half), lambda i: (i, b_col)),
            row_spec,
            pl.BlockSpec((half, D_MODEL), lambda i: (0, 0)),
            pl.BlockSpec((half, D_MODEL), lambda i: (1, 0)),
            vec_spec, mod(gt), vec_spec, mod(sc), mod(sh),
        ],
        out_specs=(row_spec, row_spec),
        compiler_params=pltpu.CompilerParams(
            dimension_semantics=("arbitrary",), vmem_limit_bytes=VMEM_LIMIT),
        name="attn_out",
    )(mixed_a, mixed_b, x2d, w_out_bf16, w_out_bf16, g_post.reshape(1, D_MODEL), gt,
      g_pre.reshape(1, D_MODEL), sc, sh)


UP_CHUNK = 1408


def _up_conv_kernel(h_ref, prev_ref, wup_ref, cw_ref, cb_ref, g_ref, st_ref, carry_ref, *, tm):
    t = pl.program_id(1)

    @pl.when(t == 0)
    def _load_state():
        carry_ref[...] = prev_ref[...]

    h = h_ref[...]
    row = lax.broadcasted_iota(jnp.int32, (tm, 1), 0)

    def conv_cols(c0):
        u = _dot(h, wup_ref[:, c0:c0 + UP_CHUNK])
        p0 = carry_ref[0:1, c0:c0 + UP_CHUNK]
        p1 = carry_ref[1:2, c0:c0 + UP_CHUNK]
        u1 = jnp.where(row == 0, p1, pltpu.roll(u, 1, 0))
        u2 = jnp.where(row == 0, p0, jnp.where(row == 1, p1, pltpu.roll(u, 2, 0)))
        w0 = cw_ref[0:1, c0:c0 + UP_CHUNK]
        w1 = cw_ref[1:2, c0:c0 + UP_CHUNK]
        w2 = cw_ref[2:3, c0:c0 + UP_CHUNK]
        conv = cb_ref[:, c0:c0 + UP_CHUNK] + w0 * u2 + w1 * u1 + w2 * u
        last = u[tm - 2:tm, :]
        carry_ref[:, c0:c0 + UP_CHUNK] = last
        st_ref[:, c0:c0 + UP_CHUNK] = last
        return conv

    for ch in range(D_FF // UP_CHUNK):
        a = conv_cols(ch * UP_CHUNK)
        b = conv_cols(D_FF + ch * UP_CHUNK)
        gate = a * (1.0 / (1.0 + jnp.exp(-a))) * b
        g_ref[:, ch * UP_CHUNK:(ch + 1) * UP_CHUNK] = gate.astype(BF16)


def _up_conv(h3d, conv_prev, w_up_bf16, conv_w, conv_b, tm):
    nb, t_len, _ = h3d.shape
    const2 = lambda b, t: (0, 0)
    return pl.pallas_call(
        functools.partial(_up_conv_kernel, tm=tm),
        out_shape=(jax.ShapeDtypeStruct((nb, t_len, D_FF), BF16),
                   jax.ShapeDtypeStruct((nb, CONV_WIDTH - 1, 2 * D_FF), F32)),
        grid=(nb, t_len // tm),
        in_specs=[
            pl.BlockSpec((None, tm, D_MODEL), lambda b, t: (b, t, 0)),
            pl.BlockSpec((None, CONV_WIDTH - 1, 2 * D_FF), lambda b, t: (b, 0, 0)),
            pl.BlockSpec((D_MODEL, 2 * D_FF), const2),
            pl.BlockSpec((CONV_WIDTH, 2 * D_FF), const2),
            pl.BlockSpec((1, 2 * D_FF), const2),
        ],
        out_specs=(pl.BlockSpec((None, tm, D_FF), lambda b, t: (b, t, 0)),
                   pl.BlockSpec((None, CONV_WIDTH - 1, 2 * D_FF), lambda b, t: (b, 0, 0))),
        scratch_shapes=[pltpu.VMEM((CONV_WIDTH - 1, 2 * D_FF), F32)],
        compiler_params=pltpu.CompilerParams(
            dimension_semantics=("arbitrary", "arbitrary"), vmem_limit_bytes=VMEM_LIMIT),
        name="up_conv_gate",
    )(h3d, conv_prev, w_up_bf16, conv_w, conv_b.reshape(1, 2 * D_FF))


def _down_kernel(g_ref, x1_ref, wd_ref, gpost_ref, gt_ref, y_ref):
    f = _dot(g_ref[...], wd_ref[...])
    y_ref[...] = x1_ref[...] + gt_ref[...] * _rms(f, gpost_ref[...], NORM_EPS)


def _down_proj(g2d, x1, w_down_bf16, g_post, gt, tm, tiles_per_group):
    n = x1.shape[0]
    row_spec = pl.BlockSpec((tm, D_MODEL), lambda i: (i, 0))
    const2 = lambda i: (0, 0)
    return pl.pallas_call(
        _down_kernel,
        out_shape=jax.ShapeDtypeStruct((n, D_MODEL), F32),
        grid=(n // tm,),
        in_specs=[
            pl.BlockSpec((tm, D_FF), lambda i: (i, 0)),
            row_spec,
            pl.BlockSpec((D_FF, D_MODEL), const2),
            pl.BlockSpec((1, D_MODEL), const2),
            _mod_spec(gt.shape[1], tiles_per_group),
        ],
        out_specs=row_spec,
        compiler_params=pltpu.CompilerParams(
            dimension_semantics=("arbitrary",), vmem_limit_bytes=VMEM_LIMIT),
        name="down_proj",
    )(g2d, x1, w_down_bf16, g_post.reshape(1, D_MODEL), gt)


def _later_mask(n):
    idx = np.arange(n)
    return (idx[None, :] > idx[:, None]).astype(np.float32)


def _alibi_slopes_np():
    return (2.0 ** (-(8.0 / N_DIFF_HEADS) * np.arange(1, N_DIFF_HEADS + 1))).astype(np.float32)


def _alibi_features(slopes2):
    bf = ml_dtypes.bfloat16
    hi = slopes2.astype(bf).astype(np.float32)
    mid = (slopes2 - hi).astype(bf).astype(np.float32)
    lo = (slopes2 - hi - mid).astype(bf).astype(np.float32)
    sfeat = np.zeros((N_DIFF_HEADS, LANES, TQ), np.float32)
    for row, piece in enumerate((hi, mid, lo)):
        sfeat[:, row, :] = piece[:, None]
    kfeat = np.zeros((TQ, LANES), np.float32)
    kfeat[:, :3] = np.arange(TQ, dtype=np.float32)[:, None]
    return (jnp.asarray(sfeat.reshape(N_DIFF_HEADS // 2, 2, LANES, TQ), dtype=BF16),
            jnp.asarray(kfeat, dtype=BF16))


def kernel(x_prompt, x_sample, c_prompt, c_sample, cache_k, cache_v, state_conv, page_table,
           w_ada, b_ada, g_pre_attn, g_post_attn, w_in, w_out, lambda_q1, lambda_k1,
           lambda_q2, lambda_k2, g_subln, g_pre_mlp, g_post_mlp, w_up, conv_w, conv_b, w_down):
    layer = 0
    n_b, seq, _ = x_prompt.shape
    n_dec, dec_t, _ = x_sample.shape
    assert dec_t == DEC_T
    n_pool = cache_k.shape[1]

    w_in_b = w_in[layer].astype(BF16)
    w_out_b = w_out[layer].astype(BF16)
    w_up_b = w_up[layer].astype(BF16)
    w_down_b = w_down[layer].astype(BF16)

    c_all = jnp.concatenate([c_prompt, c_sample], axis=0)
    mod = _modulation(c_all, w_ada[layer], b_ada[layer])
    mod_p = [m.reshape(n_b, 1, D_MODEL) for m in jnp.split(mod[:n_b], N_MOD, axis=-1)]
    mod_s = [jnp.repeat(m, DEC_T, axis=0).reshape(1, n_dec * DEC_T, D_MODEL)
             for m in jnp.split(mod[n_b:], N_MOD, axis=-1)]

    slopes2 = _alibi_slopes_np() * np.float32(LOG2E)
    slope_pairs = jnp.asarray(np.repeat(slopes2, HEAD_DIM).reshape(N_DIFF_HEADS // 2, 1, LANES))
    slope_rep = jnp.asarray(np.broadcast_to(np.repeat(slopes2, 2 * DEC_T)[:, None],
                                            (DF_ROWS, PAGE_SIZE)))
    sfeat, kfeat = _alibi_features(slopes2)
    lam_vecs = jnp.stack([lambda_q1[layer], lambda_k1[layer],
                          lambda_q2[layer], lambda_k2[layer]]).astype(F32)
    g_sub = g_subln[layer].astype(F32)
    g_col = jnp.tile(g_sub, 2).reshape(LANES, 1)
    g8 = jnp.tile(g_sub, N_DIFF_HEADS).reshape(1, DF_WIDTH)
    usuf = jnp.asarray(_later_mask(TQ), dtype=BF16)
    tri = _later_mask(PAGE_SIZE).T
    trio = jnp.asarray(np.concatenate([tri, np.ones_like(tri)], axis=1), dtype=BF16)

    tm_p = TQ
    tiles_p = seq // tm_p
    xp2d = x_prompt.reshape(n_b * seq, D_MODEL)
    sh_a, sc_a, gt_a, sh_m, sc_m, gt_m = mod_p
    qt_p, kt_p, vt_p, kb_p, vtb_p, kn2_p = _qkv_prompt(xp2d, sc_a, sh_a, g_pre_attn[layer], w_in_b,
                                                       n_b, seq, tm_p)
    kb4 = kb_p.reshape(n_b, tiles_p, TQ, D_MODEL)
    mixed_sb = _sb_prompt_attention(qt_p, kb4, vtb_p, usuf)
    mixed_df = _diff_prompt_attention(qt_p, kb4, vtb_p, slope_pairs, sfeat, kfeat,
                                      _prefix_key_norms(kn2_p, n_b, tiles_p), lam_vecs, g_col)
    x1_p, h_p = _attn_out(mixed_sb.reshape(n_b * seq, SB_WIDTH), 0,
                          mixed_df.reshape(n_b * seq, DF_WIDTH), 0,
                          xp2d, w_out_b, g_post_attn[layer], gt_a, g_pre_mlp[layer], sc_m, sh_m,
                          tm_p, tiles_p)
    conv0 = jnp.zeros((n_b, CONV_WIDTH - 1, 2 * D_FF), F32)
    gate_p, conv_p = _up_conv(h_p.reshape(n_b, seq, D_MODEL), conv0, w_up_b,
                              conv_w[layer], conv_b[layer], tm_p)
    y_p = _down_proj(gate_p.reshape(n_b * seq, D_FF), x1_p, w_down_b, g_post_mlp[layer],
                     gt_m, tm_p, tiles_p)

    tm_s = n_dec * DEC_T
    xs2d = x_sample.reshape(tm_s, D_MODEL)
    sh_a, sc_a, gt_a, sh_m, sc_m, gt_m = mod_s
    q_s, k_s, v_s = _qkv_rows(xs2d, sc_a, sh_a, g_pre_attn[layer], w_in_b, tm_s)
    per_seq = lambda a: a.reshape(n_dec, DEC_T, D_MODEL)
    pages = lambda c: c[layer].transpose(0, 2, 3, 1).reshape(n_pool, D_MODEL, PAGE_SIZE)
    mixed_s = _dec_attention(
        page_table, per_seq(q_s), per_seq(k_s), per_seq(v_s), pages(cache_k), pages(cache_v),
        trio, slope_rep, lam_vecs, g8)
    mixed_s2d = mixed_s.reshape(tm_s, D_MODEL)
    x1_s, h_s = _attn_out(mixed_s2d, 0, mixed_s2d, 1, xs2d, w_out_b, g_post_attn[layer], gt_a,
                          g_pre_mlp[layer], sc_m, sh_m, tm_s, 1)
    gate_s, conv_s = _up_conv(h_s.reshape(n_dec, DEC_T, D_MODEL), state_conv[layer].astype(F32),
                              w_up_b, conv_w[layer], conv_b[layer], DEC_T)
    y_s = _down_proj(gate_s.reshape(tm_s, D_FF), x1_s, w_down_b, g_post_mlp[layer], gt_m, tm_s, 1)

    heads = lambda a, b, t: a.reshape(1, b, t, N_HEADS, HEAD_DIM)
    heads_t = lambda a: a.reshape(1, n_b, N_HEADS, HEAD_DIM, seq).transpose(0, 1, 4, 2, 3)
    return (y_p.reshape(n_b, seq, D_MODEL),
            y_s.reshape(n_dec, DEC_T, D_MODEL),
            heads_t(kt_p), heads_t(vt_p), conv_p[None],
            heads(k_s, n_dec, DEC_T), heads(v_s, n_dec, DEC_T), conv_s[None])
```

```python
import functools
import math

import jax
import jax.numpy as jnp
import ml_dtypes
import numpy as np
from jax import lax
from jax.experimental import pallas as pl
from jax.experimental.pallas import tpu as pltpu

F32 = jnp.float32
BF16 = jnp.bfloat16

D_MODEL = 1024
HEAD_DIM = 64
N_SB_HEADS = 8
N_DIFF_HEADS = 8
N_HEADS = N_SB_HEADS + N_DIFF_HEADS
SB_WIDTH = N_SB_HEADS * HEAD_DIM
DF_WIDTH = N_DIFF_HEADS * HEAD_DIM
DIFF_QK_DIM = HEAD_DIM // 2
D_FF = 2816
CONV_WIDTH = 3
PAGE_SIZE = 128
NORM_EPS = 1e-6
SUBLN_EPS = 1e-5
N_MOD = 6
LAMBDA_INIT = 0.8 - 0.6 * math.exp(-0.3 * 0)

LANES = 128
NEG_BIG = -1e30
LOG2E = math.log2(math.e)
DEAD_LOG2 = 104.0 * LOG2E + 1.0
BOUND_SLACK = 1.001

VMEM_LIMIT = 56 * 1024 * 1024


def _dot(a, b):
    return jnp.dot(a, b, preferred_element_type=F32)


def _dot_nt(a, b):
    return lax.dot_general(a, b, (((1,), (1,)), ((), ())), preferred_element_type=F32)


def _rms(x, g, eps):
    return x * lax.rsqrt(jnp.mean(x * x, axis=-1, keepdims=True) + eps) * g


def _softplus2(z):
    return jnp.maximum(z, 0.0) + jnp.log2(1.0 + jnp.exp2(-jnp.abs(z)))


def _split_bf16(x):
    hi = x.astype(BF16)
    lo = (x - hi.astype(F32)).astype(BF16)
    return hi, lo


def _mod_kernel(c_ref, w_ref, b_ref, o_ref):
    c = c_ref[...]
    s = c * (1.0 / (1.0 + jnp.exp(-c)))
    o_ref[...] = _dot(s.astype(BF16), w_ref[...].astype(BF16)) + b_ref[...]


def _modulation(c_all, w_ada, b_ada):
    n_rows = c_all.shape[0]
    n_out = w_ada.shape[1]
    tn = 1536
    return pl.pallas_call(
        _mod_kernel,
        out_shape=jax.ShapeDtypeStruct((n_rows, n_out), F32),
        grid=(n_out // tn,),
        in_specs=[
            pl.BlockSpec((n_rows, D_MODEL), lambda j: (0, 0)),
            pl.BlockSpec((D_MODEL, tn), lambda j: (0, j)),
            pl.BlockSpec((1, tn), lambda j: (0, j)),
        ],
        out_specs=pl.BlockSpec((n_rows, tn), lambda j: (0, j)),
        compiler_params=pltpu.CompilerParams(
            dimension_semantics=("arbitrary",), vmem_limit_bytes=VMEM_LIMIT),
        name="adaln_mod",
    )(c_all, w_ada, b_ada.reshape(1, n_out))


def _scaled_qkv(x_ref, sc_ref, sh_ref, g_ref, w_ref):
    x = x_ref[...]
    h = _rms(x, g_ref[...], NORM_EPS) * (1.0 + sc_ref[...]) + sh_ref[...]
    qkv = _dot(h.astype(BF16), w_ref[...])
    lane = lax.broadcasted_iota(jnp.int32, (1, D_MODEL), 1)
    qscale = jnp.where(lane < SB_WIDTH, LOG2E * HEAD_DIM ** -0.5,
                       LOG2E * DIFF_QK_DIM ** -0.5).astype(F32)
    return qkv[:, :D_MODEL] * qscale, qkv[:, D_MODEL:2 * D_MODEL], qkv[:, 2 * D_MODEL:]


def _qkv_rows_kernel(x_ref, sc_ref, sh_ref, g_ref, w_ref, q_ref, k_ref, v_ref):
    q, k, v = _scaled_qkv(x_ref, sc_ref, sh_ref, g_ref, w_ref)
    q_ref[...] = q.astype(BF16)
    k_ref[...] = k
    v_ref[...] = v


def _qkv_prompt_kernel(x_ref, sc_ref, sh_ref, g_ref, w_ref, ind_ref,
                       qt_ref, kt_ref, vt_ref, kb_ref, vtb_ref, kn_ref):
    q, k, v = _scaled_qkv(x_ref, sc_ref, sh_ref, g_ref, w_ref)
    qt_ref[...] = q.T.astype(BF16)
    kt_ref[...] = k.T
    vt = v.T
    vt_ref[...] = vt
    vtb_ref[...] = vt.astype(BF16)
    kb = k.astype(BF16)
    kb_ref[...] = kb
    kf = kb.astype(F32)
    hi, lo = _split_bf16(kf * kf)
    ind = ind_ref[...]
    norms2 = _dot(hi, ind) + _dot(lo, ind)
    kn_ref[...] = jnp.max(norms2, axis=0, keepdims=True) * (1.0 + 2.0 ** -12)


def _mod_spec(mod_rows, tiles_per_group):
    return pl.BlockSpec((None, mod_rows, D_MODEL), lambda i: (i // tiles_per_group, 0, 0))


def _qkv_in_specs(sc, sh, tm, tiles_per_group):
    const2 = lambda i: (0, 0)
    return [
        pl.BlockSpec((tm, D_MODEL), lambda i: (i, 0)),
        _mod_spec(sc.shape[1], tiles_per_group),
        _mod_spec(sh.shape[1], tiles_per_group),
        pl.BlockSpec((1, D_MODEL), const2),
        pl.BlockSpec((D_MODEL, 3 * D_MODEL), const2),
    ]


def _qkv_rows(x2d, sc, sh, g, w_bf16, tm):
    n = x2d.shape[0]
    row_spec = pl.BlockSpec((tm, D_MODEL), lambda i: (i, 0))
    out_f32 = jax.ShapeDtypeStruct((n, D_MODEL), F32)
    return pl.pallas_call(
        _qkv_rows_kernel,
        out_shape=(jax.ShapeDtypeStruct((n, D_MODEL), BF16), out_f32, out_f32),
        grid=(n // tm,),
        in_specs=_qkv_in_specs(sc, sh, tm, 1),
        out_specs=(row_spec, row_spec, row_spec),
        compiler_params=pltpu.CompilerParams(
            dimension_semantics=("arbitrary",), vmem_limit_bytes=VMEM_LIMIT),
        name="qkv_rows",
    )(x2d, sc, sh, g.reshape(1, D_MODEL), w_bf16)


def _qkv_prompt(x2d, sc, sh, g, w_bf16, n_b, seq, tm):
    tiles = seq // tm
    t_spec = pl.BlockSpec((None, D_MODEL, tm), lambda i: (i // tiles, 0, i % tiles))
    t_f32 = jax.ShapeDtypeStruct((n_b, D_MODEL, seq), F32)
    group_of_lane = np.arange(D_MODEL)[:, None] // DIFF_QK_DIM == np.arange(LANES)[None, :]
    ind = jnp.asarray(group_of_lane.astype(np.float32), dtype=BF16)
    return pl.pallas_call(
        _qkv_prompt_kernel,
        out_shape=(jax.ShapeDtypeStruct((n_b, D_MODEL, seq), BF16), t_f32, t_f32,
                   jax.ShapeDtypeStruct((n_b * seq, D_MODEL), BF16),
                   jax.ShapeDtypeStruct((n_b, tiles, D_MODEL, tm), BF16),
                   jax.ShapeDtypeStruct((n_b * tiles, 1, LANES), F32)),
        grid=(n_b * tiles,),
        in_specs=_qkv_in_specs(sc, sh, tm, tiles) + [pl.BlockSpec((D_MODEL, LANES), lambda i: (0, 0))],
        out_specs=(t_spec, t_spec, t_spec,
                   pl.BlockSpec((tm, D_MODEL), lambda i: (i, 0)),
                   pl.BlockSpec((None, None, D_MODEL, tm), lambda i: (i // tiles, i % tiles, 0, 0)),
                   pl.BlockSpec((None, 1, LANES), lambda i: (i, 0, 0))),
        compiler_params=pltpu.CompilerParams(
            dimension_semantics=("arbitrary",), vmem_limit_bytes=VMEM_LIMIT),
        name="qkv_prompt",
    )(x2d, sc, sh, g.reshape(1, D_MODEL), w_bf16, ind)


def _prefix_key_norms(kn2, n_b, tiles):
    n_groups = 2 * N_DIFF_HEADS
    first = SB_WIDTH // DIFF_QK_DIM
    kn = jnp.sqrt(kn2.reshape(n_b, tiles, LANES)[:, :, first:first + n_groups])
    kn = lax.cummax(kn, axis=1)
    kn = kn.reshape(n_b, tiles, N_DIFF_HEADS // 2, 4).transpose(0, 2, 1, 3)
    return jnp.pad(kn, ((0, 0), (0, 0), (0, 0), (0, LANES - 4)))


def _lambda_value(lam_ref):
    lq1 = lam_ref[0:1, :]
    lk1 = lam_ref[1:2, :]
    lq2 = lam_ref[2:3, :]
    lk2 = lam_ref[3:4, :]
    return (jnp.exp(jnp.sum(lq1 * lk1, axis=1, keepdims=True))
            - jnp.exp(jnp.sum(lq2 * lk2, axis=1, keepdims=True)) + LAMBDA_INIT)


TQ = 256
SB_STEP_HEADS = 4
SB_STEP_WIDTH = SB_STEP_HEADS * HEAD_DIM
ONES_ROWS = 16
DIFF_TRIPS = ((1, 1), (2, 1), (4, 1), (8, 1 << 30), (4, 1), (2, 1), (1, 1))


def _prompt_specs(s, first_group, width):
    n_blk = s // TQ
    return [
        pl.BlockSpec((None, width, TQ), lambda bi, p, i: (bi, first_group + p, i)),
        pl.BlockSpec((None, n_blk, TQ, width), lambda bi, p, i: (bi, 0, 0, first_group + p)),
        pl.BlockSpec((None, n_blk, width, TQ), lambda bi, p, i: (bi, 0, first_group + p, 0)),
    ]


def _sb_prompt_kernel(qt_ref, k_ref, vt_ref, usuf_ref, o_ref, acc_ref):
    i = pl.program_id(2)
    heads = range(SB_STEP_HEADS)
    qt = qt_ref[...].astype(F32)
    drow = lax.broadcasted_iota(jnp.int32, (SB_STEP_WIDTH, TQ), 0)
    key = lax.broadcasted_iota(jnp.int32, (TQ, TQ), 0)
    qry = lax.broadcasted_iota(jnp.int32, (TQ, TQ), 1)
    strictly_causal = key < qry
    usuf = usuf_ref[...]
    qtm = [jnp.where((drow >= HEAD_DIM * h) & (drow < HEAD_DIM * (h + 1)), qt, 0.0).astype(BF16)
           for h in heads]

    def step(js, runs, masked_first):
        streams = [(b, h) for b in range(len(js)) for h in heads]
        kbs = [k_ref[j] for j in js]
        vts = [vt_ref[j] for j in js]
        zs = {(b, h): _dot(kbs[b], qtm[h]) for b, h in streams}
        log_sig, parts, offsets = {}, {}, {}
        runs = list(runs)
        for b, h in streams:
            masked = masked_first and b == 0
            sp = _softplus2(zs[b, h])
            log_sig[b, h] = zs[b, h] - sp
            if masked:
                sp = jnp.where(strictly_causal, sp, 0.0)
            parts[b, h] = _split_bf16(sp)
            offsets[b, h] = runs[h]
            runs[h] = runs[h] + jnp.sum(sp, axis=0, keepdims=True)
        between = {s: _dot(usuf, parts[s][0]) + _dot(usuf, parts[s][1]) for s in streams}
        weights = {}
        for b, h in streams:
            w = jnp.exp2(log_sig[b, h] - between[b, h] - offsets[b, h])
            if masked_first and b == 0:
                w = jnp.where(strictly_causal, w, 0.0)
            weights[b, h] = w.astype(BF16)
        outs = {(b, h): _dot(vts[b][HEAD_DIM * h:HEAD_DIM * (h + 1)], weights[b, h])
                for b, h in streams}
        for h in heads:
            acc = acc_ref[h]
            for b in range(len(js)):
                acc = acc + outs[b, h]
            acc_ref[h] = acc
        return tuple(runs)

    def alive(runs):
        least = runs[0]
        for r in runs[1:]:
            least = jnp.minimum(least, r)
        return (jnp.min(least) < DEAD_LOG2).astype(jnp.int32)

    acc_ref[...] = jnp.zeros_like(acc_ref)
    zeros = (jnp.zeros((1, TQ), F32),) * SB_STEP_HEADS
    runs = lax.cond(i > 0,
                    lambda: step([i, i - 1], zeros, True),
                    lambda: step([i], zeros, True))

    def cond(c):
        return (c[0] < i) & (c[1] > 0)

    def body(c):
        runs = step([i - 1 - c[0]], c[2:], False)
        return (c[0] + 1, alive(runs)) + runs

    lax.while_loop(cond, body, (jnp.int32(1), alive(runs)) + runs)
    out_t = jnp.concatenate([acc_ref[h] for h in heads], axis=0)
    o_ref[...] = out_t.T.astype(BF16)


def _sb_prompt_attention(qt, kb4, vtb4, usuf):
    b, _, s = qt.shape
    return pl.pallas_call(
        _sb_prompt_kernel,
        out_shape=jax.ShapeDtypeStruct((b, s, SB_WIDTH), BF16),
        grid=(b, SB_WIDTH // SB_STEP_WIDTH, s // TQ),
        in_specs=_prompt_specs(s, 0, SB_STEP_WIDTH)
        + [pl.BlockSpec((TQ, TQ), lambda bi, p, i: (0, 0))],
        out_specs=pl.BlockSpec((None, TQ, SB_STEP_WIDTH), lambda bi, p, i: (bi, i, p)),
        scratch_shapes=[pltpu.VMEM((SB_STEP_HEADS, HEAD_DIM, TQ), F32)],
        compiler_params=pltpu.CompilerParams(
            dimension_semantics=("arbitrary", "arbitrary", "arbitrary"),
            vmem_limit_bytes=VMEM_LIMIT),
        name="sb_prompt_attn",
    )(qt, kb4, vtb4, usuf)


def _diff_prompt_kernel(qt_ref, k_ref, vt_ref, slope_ref, sfeat_ref, kfeat_ref, kmax_ref, lam_ref,
                        g_ref, o_ref, acc_ref):
    i = pl.program_id(2)
    qt = qt_ref[...].astype(F32)
    drow = lax.broadcasted_iota(jnp.int32, (LANES, TQ), 0)
    key = lax.broadcasted_iota(jnp.int32, (TQ, TQ), 0)
    qry = lax.broadcasted_iota(jnp.int32, (TQ, TQ), 1)
    causal = key <= qry
    lam = _lambda_value(lam_ref)
    slopes = [slope_ref[:, HEAD_DIM * h:HEAD_DIM * h + 1] for h in range(2)]
    kfeat = kfeat_ref[...]
    qext = []
    for h in range(2):
        for c in range(2):
            lo_row = HEAD_DIM * h + DIFF_QK_DIM * c
            in_map = (drow >= lo_row) & (drow < lo_row + DIFF_QK_DIM)
            qtm = jnp.where(in_map, qt, 0.0).astype(BF16)
            qext.append(jnp.concatenate([qtm, sfeat_ref[h]], axis=0))

    ones_rows = jnp.ones((ONES_ROWS, TQ), BF16)

    def step(js, ms, masked):
        ms = list(ms)
        kexts = [jnp.concatenate([k_ref[j], kfeat], axis=1) for j in js]
        scores = [[_dot(kext, qext[idx]) for idx in range(4)] for kext in kexts]
        pending = []
        for b, j in enumerate(js):
            alphas, probs = [], []
            for idx in range(4):
                h = idx // 2
                s = scores[b][idx]
                if masked:
                    s = jnp.where(causal, s, NEG_BIG)
                shift = slopes[h] * ((i - j) * TQ).astype(F32)
                m_new = jnp.maximum(ms[idx], jnp.max(s, axis=0, keepdims=True) - shift)
                alphas.append(jnp.exp2(ms[idx] - m_new))
                probs.append(jnp.exp2(s - (m_new + shift)).astype(BF16))
                ms[idx] = m_new
            vt = vt_ref[j]
            vts = [jnp.concatenate([vt[HEAD_DIM * h:HEAD_DIM * (h + 1)], ones_rows], axis=0)
                   for h in range(2)]
            pending.append((alphas, [_dot(vts[idx // 2], probs[idx]) for idx in range(4)]))
        for idx in range(4):
            acc = acc_ref[idx]
            for alphas, outs in pending:
                acc = alphas[idx] * acc + outs[idx]
            acc_ref[idx] = acc
        return tuple(ms)

    q_norm = []
    for idx in range(4):
        lo_row = DIFF_QK_DIM * idx
        in_map = (drow >= lo_row) & (drow < lo_row + DIFF_QK_DIM)
        q_norm.append(jnp.sqrt(jnp.sum(jnp.where(in_map, qt * qt, 0.0), axis=0, keepdims=True)))

    def alive(rem, ms):
        k_norm = kmax_ref[pl.ds(jnp.maximum(rem - 1, 0), 1), :]
        gap = ((TQ - 1) - (i - rem + 1) * TQ).astype(F32)
        worst = jnp.full((1, TQ), NEG_BIG, F32)
        for idx in range(4):
            bound = q_norm[idx] * k_norm[:, idx:idx + 1] * BOUND_SLACK + slopes[idx // 2] * gap
            worst = jnp.maximum(worst, bound - ms[idx])
        return (jnp.max(worst) > -DEAD_LOG2).astype(jnp.int32)

    acc_ref[...] = jnp.zeros_like(acc_ref)
    ms = step([i], (jnp.full((1, TQ), NEG_BIG, F32),) * 4, True)

    state = (i, alive(i, ms)) + ms
    for group, max_trips in DIFF_TRIPS:
        def cond(c, group=group, max_trips=max_trips):
            return (c[0] < max_trips) & (c[1] >= group) & (c[2] > 0)

        def body(c, group=group):
            rem, ms = c[1], c[3:]
            still = alive(rem - group, ms)
            ms = step([rem - group + u for u in range(group)], ms, False)
            return (c[0] + 1, rem - group, still) + ms

        state = lax.while_loop(cond, body, (jnp.int32(0),) + state)[1:]

    y_t = []
    for h in range(2):
        maps = [acc_ref[2 * h + c, :HEAD_DIM, :] / acc_ref[2 * h + c, HEAD_DIM:HEAD_DIM + 1, :]
                for c in range(2)]
        o_t = maps[0] - lam * maps[1]
        ms_t = jnp.mean(o_t * o_t, axis=0, keepdims=True)
        y_t.append(o_t * lax.rsqrt(ms_t + SUBLN_EPS) * g_ref[...] * (1.0 - LAMBDA_INIT))
    o_ref[...] = jnp.concatenate(y_t, axis=0).T.astype(BF16)


def _diff_prompt_attention(qt, kb4, vtb4, slopes, sfeat, kfeat, kmax, lam_vecs, g_col):
    b, _, s = qt.shape
    n_pairs = DF_WIDTH // LANES
    const2 = lambda bi, p, i: (0, 0)
    return pl.pallas_call(
        _diff_prompt_kernel,
        out_shape=jax.ShapeDtypeStruct((b, s, DF_WIDTH), BF16),
        grid=(b, n_pairs, s // TQ),
        in_specs=_prompt_specs(s, SB_WIDTH // LANES, LANES) + [
            pl.BlockSpec((None, 1, LANES), lambda bi, p, i: (p, 0, 0)),
            pl.BlockSpec((None, 2, LANES, TQ), lambda bi, p, i: (p, 0, 0, 0)),
            pl.BlockSpec((TQ, LANES), const2),
            pl.BlockSpec((None, None, s // TQ, LANES), lambda bi, p, i: (bi, p, 0, 0)),
            pl.BlockSpec((4, DIFF_QK_DIM), const2),
            pl.BlockSpec((HEAD_DIM, 1), const2),
        ],
        out_specs=pl.BlockSpec((None, TQ, LANES), lambda bi, p, i: (bi, i, p)),
        scratch_shapes=[pltpu.VMEM((4, HEAD_DIM + ONES_ROWS, TQ), F32)],
        compiler_params=pltpu.CompilerParams(
            dimension_semantics=("arbitrary", "arbitrary", "arbitrary"),
            vmem_limit_bytes=VMEM_LIMIT),
        name="diff_prompt_attn",
    )(qt, kb4, vtb4, slopes, sfeat, kfeat, kmax, lam_vecs, g_col)


PAGES_PER_STEP = 8
DEC_T = 8
SB_ROWS = N_SB_HEADS * DEC_T
DF_ROWS = N_DIFF_HEADS * 2 * DEC_T


def _group_rms_lanes(o, g, lane, n_groups):
    sq = o * o
    ms = jnp.zeros_like(o)
    for h in range(n_groups):
        in_h = (lane >= HEAD_DIM * h) & (lane < HEAD_DIM * (h + 1))
        s_h = jnp.sum(jnp.where(in_h, sq, 0.0), axis=1, keepdims=True)
        ms = jnp.where(in_h, s_h * (1.0 / HEAD_DIM), ms)
    return o * lax.rsqrt(ms + SUBLN_EPS) * g


def _dec_attn_kernel(pt_ref, q_ref, kn_ref, vn_ref, *rest, past_len):
    del pt_ref
    kp_refs = rest[:PAGES_PER_STEP]
    vp_refs = rest[PAGES_PER_STEP:2 * PAGES_PER_STEP]
    (trio_ref, slope_ref, lam_ref, g_ref, o_ref, wsb_ref, wdf_ref,
     run_ref, accsb_ref, mcol_ref, mrep_ref, l_ref, accdf_ref) = rest[2 * PAGES_PER_STEP:]
    step_id = pl.program_id(1)
    n_steps = pl.num_programs(1)

    key_lane = lax.broadcasted_iota(jnp.int32, (DF_ROWS, PAGE_SIZE), 1)
    t_df = lax.broadcasted_iota(jnp.int32, (DF_ROWS, PAGE_SIZE), 0) & (DEC_T - 1)
    slope = slope_ref[...]
    bias0 = slope * (key_lane - past_len - t_df).astype(F32)
    trio = trio_ref[...]

    def sb_group(chunks):
        offset = run_ref[...]
        acc = accsb_ref[...]
        parts = []
        for z, _, mask in chunks:
            sp = _softplus2(z)
            log_sig = z - sp
            if mask is not None:
                sp = jnp.where(mask, sp, 0.0)
            parts.append((log_sig, _split_bf16(sp)))
        sums = [_dot(hi, trio) + _dot(lo, trio) for _, (hi, lo) in parts]
        for (_, pv, mask), (log_sig, _), sm in zip(chunks, parts, sums):
            w = jnp.exp2(log_sig - sm[:, :PAGE_SIZE] - offset)
            if mask is not None:
                w = jnp.where(mask, w, 0.0)
            acc = acc + pv(w.astype(BF16))
            offset = offset + sm[:, PAGE_SIZE:]
        run_ref[...] = offset
        accsb_ref[...] = acc

    def df_group(chunks):
        top = chunks[0][0]
        for s, _ in chunks[1:]:
            top = jnp.maximum(top, s)
        m_old = mcol_ref[...]
        m_new = jnp.maximum(m_old, jnp.max(top, axis=1, keepdims=True))
        m_rep_old = mrep_ref[...]
        m_rep = jnp.broadcast_to(m_new, (DF_ROWS, PAGE_SIZE))
        p_sum = jnp.zeros((DF_ROWS, PAGE_SIZE), F32)
        pv_sum = jnp.zeros((DF_ROWS, DF_WIDTH), F32)
        for s, pv in chunks:
            p = jnp.exp2(s - m_rep)
            p_sum = p_sum + p
            pv_sum = pv_sum + pv(p.astype(BF16))
        alpha_rep = jnp.exp2(m_rep_old - m_rep)
        l_ref[...] = jnp.exp2(m_old - m_new) * l_ref[...] + jnp.sum(p_sum, axis=1, keepdims=True)
        accdf_ref[...] = jnp.concatenate([alpha_rep] * (DF_WIDTH // PAGE_SIZE), axis=1) * accdf_ref[...] + pv_sum
        mcol_ref[...] = m_new
        mrep_ref[...] = m_rep

    @pl.when(step_id == 0)
    def _init():
        qf = q_ref[...].astype(F32)
        q_sb = jnp.concatenate([qf[:, :SB_WIDTH]] * N_SB_HEADS, axis=0)
        r = lax.broadcasted_iota(jnp.int32, (SB_ROWS, SB_WIDTH), 0)
        c = lax.broadcasted_iota(jnp.int32, (SB_ROWS, SB_WIDTH), 1)
        wsb_ref[...] = jnp.where((r >> 3) == (c >> 6), q_sb, 0.0).astype(BF16)
        q_df = jnp.concatenate([qf[:, SB_WIDTH:]] * (2 * N_DIFF_HEADS), axis=0)
        r = lax.broadcasted_iota(jnp.int32, (DF_ROWS, DF_WIDTH), 0)
        c = lax.broadcasted_iota(jnp.int32, (DF_ROWS, DF_WIDTH), 1)
        wdf_ref[...] = jnp.where((r >> 3) == (c >> 5), q_df, 0.0).astype(BF16)
        run_ref[...] = jnp.zeros_like(run_ref)
        accsb_ref[...] = jnp.zeros_like(accsb_ref)
        mcol_ref[...] = jnp.full_like(mcol_ref, NEG_BIG)
        mrep_ref[...] = jnp.full_like(mrep_ref, NEG_BIG)
        l_ref[...] = jnp.zeros_like(l_ref)
        accdf_ref[...] = jnp.zeros_like(accdf_ref)
        pad = jnp.zeros((PAGE_SIZE - DEC_T, D_MODEL), F32)
        kn = jnp.concatenate([kn_ref[...], pad], axis=0).astype(BF16)
        vn = jnp.concatenate([vn_ref[...], pad], axis=0).astype(BF16)
        key_sb = lax.broadcasted_iota(jnp.int32, (SB_ROWS, PAGE_SIZE), 1)
        t_sb = lax.broadcasted_iota(jnp.int32, (SB_ROWS, PAGE_SIZE), 0) & (DEC_T - 1)
        sb_group([(_dot_nt(wsb_ref[...], kn[:, :SB_WIDTH]),
                   lambda w: _dot(w, vn[:, :SB_WIDTH]), key_sb < t_sb)])
        s_new = _dot_nt(wdf_ref[...], kn[:, SB_WIDTH:]) + (bias0 + slope * float(past_len))
        df_group([(jnp.where(key_lane <= t_df, s_new, NEG_BIG), lambda p: _dot(p, vn[:, SB_WIDTH:]))])

    n_pages = n_steps * PAGES_PER_STEP
    wsb = wsb_ref[...]
    wdf = wdf_ref[...]
    sb_chunks, df_chunks = [], []
    for r in range(PAGES_PER_STEP):
        page = n_pages - 1 - (step_id * PAGES_PER_STEP + r)
        kp = kp_refs[r][...].astype(BF16)
        vp = vp_refs[r][...].astype(BF16)
        sb_chunks.append((_dot(wsb, kp[:SB_WIDTH, :]),
                          functools.partial(_dot_nt, b=vp[:SB_WIDTH, :]), None))
        base = (page * PAGE_SIZE).astype(F32)
        df_chunks.append((_dot(wdf, kp[SB_WIDTH:, :]) + (bias0 + slope * base),
                          functools.partial(_dot_nt, b=vp[SB_WIDTH:, :])))
    sb_group(sb_chunks)
    df_group(df_chunks)

    @pl.when(step_id == n_steps - 1)
    def _finish():
        lane = lax.broadcasted_iota(jnp.int32, (DEC_T, SB_WIDTH), 1)
        lam = _lambda_value(lam_ref)
        acc_sb = accsb_ref[...]
        acc_df = accdf_ref[...] / l_ref[...]
        o_sb = jnp.zeros((DEC_T, SB_WIDTH), F32)
        o_df = jnp.zeros((DEC_T, DF_WIDTH), F32)
        for h in range(N_SB_HEADS):
            in_h = (lane >> 6) == h
            o_sb = jnp.where(in_h, acc_sb[DEC_T * h:DEC_T * (h + 1), :], o_sb)
            r0 = 2 * DEC_T * h
            o_h = acc_df[r0:r0 + DEC_T, :] - lam * acc_df[r0 + DEC_T:r0 + 2 * DEC_T, :]
            o_df = jnp.where(in_h, o_h, o_df)
        y_df = _group_rms_lanes(o_df, g_ref[...], lane, N_DIFF_HEADS) * (1.0 - LAMBDA_INIT)
        o_ref[:, :SB_WIDTH] = o_sb.astype(BF16)
        o_ref[:, SB_WIDTH:] = y_df.astype(BF16)


def _dec_attention(page_table, q, k_new, v_new, cache_k, cache_v, trio, slope_rep, lam_vecs, g8):
    n_seq, n_pages = page_table.shape
    past_len = n_pages * PAGE_SIZE
    n_steps = n_pages // PAGES_PER_STEP

    def page_spec(r):
        def idx(b, s, pt):
            return (pt[b, n_pages - 1 - (s * PAGES_PER_STEP + r)], 0, 0)
        return pl.BlockSpec((None, D_MODEL, PAGE_SIZE), idx)

    seq_spec = pl.BlockSpec((None, DEC_T, D_MODEL), lambda b, s, pt: (b, 0, 0))
    const2 = lambda b, s, pt: (0, 0)
    grid_spec = pltpu.PrefetchScalarGridSpec(
        num_scalar_prefetch=1,
        grid=(n_seq, n_steps),
        in_specs=[seq_spec, seq_spec, seq_spec]
        + [page_spec(r) for r in range(PAGES_PER_STEP)]
        + [page_spec(r) for r in range(PAGES_PER_STEP)]
        + [
            pl.BlockSpec((PAGE_SIZE, 2 * PAGE_SIZE), const2),
            pl.BlockSpec((DF_ROWS, PAGE_SIZE), const2),
            pl.BlockSpec((4, DIFF_QK_DIM), const2),
            pl.BlockSpec((1, DF_WIDTH), const2),
        ],
        out_specs=seq_spec,
        scratch_shapes=[
            pltpu.VMEM((SB_ROWS, SB_WIDTH), BF16),
            pltpu.VMEM((DF_ROWS, DF_WIDTH), BF16),
            pltpu.VMEM((SB_ROWS, PAGE_SIZE), F32),
            pltpu.VMEM((SB_ROWS, SB_WIDTH), F32),
            pltpu.VMEM((DF_ROWS, 1), F32),
            pltpu.VMEM((DF_ROWS, PAGE_SIZE), F32),
            pltpu.VMEM((DF_ROWS, 1), F32),
            pltpu.VMEM((DF_ROWS, DF_WIDTH), F32),
        ],
    )
    return pl.pallas_call(
        functools.partial(_dec_attn_kernel, past_len=past_len),
        out_shape=jax.ShapeDtypeStruct((n_seq, DEC_T, D_MODEL), BF16),
        grid_spec=grid_spec,
        compiler_params=pltpu.CompilerParams(
            dimension_semantics=("arbitrary", "arbitrary"), vmem_limit_bytes=VMEM_LIMIT),
        name="dec_attn",
    )(page_table, q, k_new, v_new, *([cache_k] * PAGES_PER_STEP), *([cache_v] * PAGES_PER_STEP),
      trio, slope_rep, lam_vecs, g8)


def _attn_out_kernel(ma_ref, mb_ref, x_ref, wa_ref, wb_ref, gpost_ref, gt_ref,
                     gpre_ref, sc_ref, sh_ref, x1_ref, h_ref):
    y = _dot(ma_ref[...], wa_ref[...]) + _dot(mb_ref[...], wb_ref[...])
    x1 = x_ref[...] + gt_ref[...] * _rms(y, gpost_ref[...], NORM_EPS)
    x1_ref[...] = x1
    h = _rms(x1, gpre_ref[...], NORM_EPS) * (1.0 + sc_ref[...]) + sh_ref[...]
    h_ref[...] = h.astype(BF16)


def _attn_out(mixed_a, a_col, mixed_b, b_col, x2d, w_out_bf16, g_post, gt, g_pre, sc, sh,
              tm, tiles_per_group):
    n = x2d.shape[0]
    half = D_MODEL // 2
    row_spec = pl.BlockSpec((tm, D_MODEL), lambda i: (i, 0))
    const2 = lambda i: (0, 0)
    vec_spec = pl.BlockSpec((1, D_MODEL), const2)
    mod = lambda a: _mod_spec(a.shape[1], tiles_per_group)
    return pl.pallas_call(
        _attn_out_kernel,
        out_shape=(jax.ShapeDtypeStruct((n, D_MODEL), F32),
                   jax.ShapeDtypeStruct((n, D_MODEL), BF16)),
        grid=(n // tm,),
        in_specs=[
            pl.BlockSpec((tm, half), lambda i: (i, a_col)),
            pl.BlockSpec((tm, half), lambda i: (i, b_col)),
            row_spec,
            pl.BlockSpec((half, D_MODEL), lambda i: (0, 0)),
            pl.BlockSpec((half, D_MODEL), lambda i: (1, 0)),
            vec_spec, mod(gt), vec_spec, mod(sc), mod(sh),
        ],
        out_specs=(row_spec, row_spec),
        compiler_params=pltpu.CompilerParams(
            dimension_semantics=("arbitrary",), vmem_limit_bytes=VMEM_LIMIT),
        name="attn_out",
    )(mixed_a, mixed_b, x2d, w_out_bf16, w_out_bf16, g_post.reshape(1, D_MODEL), gt,
      g_pre.reshape(1, D_MODEL), sc, sh)


UP_CHUNK = 1408


def _up_conv_kernel(h_ref, prev_ref, wup_ref, cw_ref, cb_ref, g_ref, st_ref, carry_ref, *, tm):
    t = pl.program_id(1)

    @pl.when(t == 0)
    def _load_state():
        carry_ref[...] = prev_ref[...]

    h = h_ref[...]
    row = lax.broadcasted_iota(jnp.int32, (tm, 1), 0)

    def conv_cols(c0):
        u = _dot(h, wup_ref[:, c0:c0 + UP_CHUNK])
        p0 = carry_ref[0:1, c0:c0 + UP_CHUNK]
        p1 = carry_ref[1:2, c0:c0 + UP_CHUNK]
        u1 = jnp.where(row == 0, p1, pltpu.roll(u, 1, 0))
        u2 = jnp.where(row == 0, p0, jnp.where(row == 1, p1, pltpu.roll(u, 2, 0)))
        w0 = cw_ref[0:1, c0:c0 + UP_CHUNK]
        w1 = cw_ref[1:2, c0:c0 + UP_CHUNK]
        w2 = cw_ref[2:3, c0:c0 + UP_CHUNK]
        conv = cb_ref[:, c0:c0 + UP_CHUNK] + w0 * u2 + w1 * u1 + w2 * u
        last = u[tm - 2:tm, :]
        carry_ref[:, c0:c0 + UP_CHUNK] = last
        st_ref[:, c0:c0 + UP_CHUNK] = last
        return conv

    for ch in range(D_FF // UP_CHUNK):
        a = conv_cols(ch * UP_CHUNK)
        b = conv_cols(D_FF + ch * UP_CHUNK)
        gate = a * (1.0 / (1.0 + jnp.exp(-a))) * b
        g_ref[:, ch * UP_CHUNK:(ch + 1) * UP_CHUNK] = gate.astype(BF16)


def _up_conv(h3d, conv_prev, w_up_bf16, conv_w, conv_b, tm):
    nb, t_len, _ = h3d.shape
    const2 = lambda b, t: (0, 0)
    return pl.pallas_call(
        functools.partial(_up_conv_kernel, tm=tm),
        out_shape=(jax.ShapeDtypeStruct((nb, t_len, D_FF), BF16),
                   jax.ShapeDtypeStruct((nb, CONV_WIDTH - 1, 2 * D_FF), F32)),
        grid=(nb, t_len // tm),
        in_specs=[
            pl.BlockSpec((None, tm, D_MODEL), lambda b, t: (b, t, 0)),
            pl.BlockSpec((None, CONV_WIDTH - 1, 2 * D_FF), lambda b, t: (b, 0, 0)),
            pl.BlockSpec((D_MODEL, 2 * D_FF), const2),
            pl.BlockSpec((CONV_WIDTH, 2 * D_FF), const2),
            pl.BlockSpec((1, 2 * D_FF), const2),
        ],
        out_specs=(pl.BlockSpec((None, tm, D_FF), lambda b, t: (b, t, 0)),
                   pl.BlockSpec((None, CONV_WIDTH - 1, 2 * D_FF), lambda b, t: (b, 0, 0))),
        scratch_shapes=[pltpu.VMEM((CONV_WIDTH - 1, 2 * D_FF), F32)],
        compiler_params=pltpu.CompilerParams(
            dimension_semantics=("arbitrary", "arbitrary"), vmem_limit_bytes=VMEM_LIMIT),
        name="up_conv_gate",
    )(h3d, conv_prev, w_up_bf16, conv_w, conv_b.reshape(1, 2 * D_FF))


def _up_conv_rows_kernel(h_ref, pa_ref, pb_ref, wa_ref, wb_ref, cwa_ref, cwb_ref, cba_ref, cbb_ref,
                         g_ref, ua_ref, ub_ref):
    h = h_ref[...]
    rows = h.shape[0]
    t = lax.broadcasted_iota(jnp.int32, (rows, 1), 0) & (DEC_T - 1)

    def conv(w_ref, prev_ref, cw_ref, cb_ref, u_ref):
        u = _dot(h, w_ref[...])
        u_ref[...] = u
        prev = prev_ref[...]
        u1 = jnp.where(t == 0, pltpu.roll(prev, rows - 1, 0), pltpu.roll(u, 1, 0))
        u2 = jnp.where(t < 2, prev, pltpu.roll(u, 2, 0))
        return cb_ref[...] + cw_ref[0:1, :] * u2 + cw_ref[1:2, :] * u1 + cw_ref[2:3, :] * u

    a = conv(wa_ref, pa_ref, cwa_ref, cba_ref, ua_ref)
    b = conv(wb_ref, pb_ref, cwb_ref, cbb_ref, ub_ref)
    g_ref[...] = (a * (1.0 / (1.0 + jnp.exp(-a))) * b).astype(BF16)


def _up_conv_rows(h2d, prev2d, w_up_bf16, conv_w, conv_b):
    rows = h2d.shape[0]
    n_chunks = D_FF // UP_CHUNK
    half_a = lambda r: pl.BlockSpec((r, UP_CHUNK), lambda c: (0, c))
    half_b = lambda r: pl.BlockSpec((r, UP_CHUNK), lambda c: (0, n_chunks + c))
    u_half = jax.ShapeDtypeStruct((rows, D_FF), F32)
    cb2d = conv_b.reshape(1, 2 * D_FF)
    return pl.pallas_call(
        _up_conv_rows_kernel,
        out_shape=(jax.ShapeDtypeStruct((rows, D_FF), BF16), u_half, u_half),
        grid=(n_chunks,),
        in_specs=[pl.BlockSpec((rows, D_MODEL), lambda c: (0, 0)),
                  half_a(rows), half_b(rows), half_a(D_MODEL), half_b(D_MODEL),
                  half_a(CONV_WIDTH), half_b(CONV_WIDTH), half_a(1), half_b(1)],
        out_specs=(half_a(rows), half_a(rows), half_a(rows)),
        compiler_params=pltpu.CompilerParams(
            dimension_semantics=("arbitrary",), vmem_limit_bytes=VMEM_LIMIT),
        name="up_conv_rows",
    )(h2d, prev2d, prev2d, w_up_bf16, w_up_bf16, conv_w, conv_w, cb2d, cb2d)


def _down_kernel(g_ref, x1_ref, wd_ref, gpost_ref, gt_ref, y_ref):
    f = _dot(g_ref[...], wd_ref[...])
    y_ref[...] = x1_ref[...] + gt_ref[...] * _rms(f, gpost_ref[...], NORM_EPS)


def _down_proj(g2d, x1, w_down_bf16, g_post, gt, tm, tiles_per_group):
    n = x1.shape[0]
    row_spec = pl.BlockSpec((tm, D_MODEL), lambda i: (i, 0))
    const2 = lambda i: (0, 0)
    return pl.pallas_call(
        _down_kernel,
        out_shape=jax.ShapeDtypeStruct((n, D_MODEL), F32),
        grid=(n // tm,),
        in_specs=[
            pl.BlockSpec((tm, D_FF), lambda i: (i, 0)),
            row_spec,
            pl.BlockSpec((D_FF, D_MODEL), const2),
            pl.BlockSpec((1, D_MODEL), const2),
            _mod_spec(gt.shape[1], tiles_per_group),
        ],
        out_specs=row_spec,
        compiler_params=pltpu.CompilerParams(
            dimension_semantics=("arbitrary",), vmem_limit_bytes=VMEM_LIMIT),
        name="down_proj",
    )(g2d, x1, w_down_bf16, g_post.reshape(1, D_MODEL), gt)


def _later_mask(n):
    idx = np.arange(n)
    return (idx[None, :] > idx[:, None]).astype(np.float32)


def _alibi_slopes_np():
    return (2.0 ** (-(8.0 / N_DIFF_HEADS) * np.arange(1, N_DIFF_HEADS + 1))).astype(np.float32)


def _alibi_features(slopes2):
    bf = ml_dtypes.bfloat16
    hi = slopes2.astype(bf).astype(np.float32)
    mid = (slopes2 - hi).astype(bf).astype(np.float32)
    lo = (slopes2 - hi - mid).astype(bf).astype(np.float32)
    sfeat = np.zeros((N_DIFF_HEADS, LANES, TQ), np.float32)
    for row, piece in enumerate((hi, mid, lo)):
        sfeat[:, row, :] = piece[:, None]
    kfeat = np.zeros((TQ, LANES), np.float32)
    kfeat[:, :3] = np.arange(TQ, dtype=np.float32)[:, None]
    return (jnp.asarray(sfeat.reshape(N_DIFF_HEADS // 2, 2, LANES, TQ), dtype=BF16),
            jnp.asarray(kfeat, dtype=BF16))


def kernel(x_prompt, x_sample, c_prompt, c_sample, cache_k, cache_v, state_conv, page_table,
           w_ada, b_ada, g_pre_attn, g_post_attn, w_in, w_out, lambda_q1, lambda_k1,
           lambda_q2, lambda_k2, g_subln, g_pre_mlp, g_post_mlp, w_up, conv_w, conv_b, w_down):
    layer = 0
    n_b, seq, _ = x_prompt.shape
    n_dec, dec_t, _ = x_sample.shape
    assert dec_t == DEC_T
    n_pool = cache_k.shape[1]

    w_in_b = w_in[layer].astype(BF16)
    w_out_b = w_out[layer].astype(BF16)
    w_up_b = w_up[layer].astype(BF16)
    w_down_b = w_down[layer].astype(BF16)

    c_all = jnp.concatenate([c_prompt, c_sample], axis=0)
    mod = _modulation(c_all, w_ada[layer], b_ada[layer])
    mod_p = [m.reshape(n_b, 1, D_MODEL) for m in jnp.split(mod[:n_b], N_MOD, axis=-1)]
    mod_s = [jnp.repeat(m, DEC_T, axis=0).reshape(1, n_dec * DEC_T, D_MODEL)
             for m in jnp.split(mod[n_b:], N_MOD, axis=-1)]

    slopes2 = _alibi_slopes_np() * np.float32(LOG2E)
    slope_pairs = jnp.asarray(np.repeat(slopes2, HEAD_DIM).reshape(N_DIFF_HEADS // 2, 1, LANES))
    slope_rep = jnp.asarray(np.broadcast_to(np.repeat(slopes2, 2 * DEC_T)[:, None],
                                            (DF_ROWS, PAGE_SIZE)))
    sfeat, kfeat = _alibi_features(slopes2)
    lam_vecs = jnp.stack([lambda_q1[layer], lambda_k1[layer],
                          lambda_q2[layer], lambda_k2[layer]]).astype(F32)
    g_sub = g_subln[layer].astype(F32)
    g_col = g_sub.reshape(HEAD_DIM, 1)
    g8 = jnp.tile(g_sub, N_DIFF_HEADS).reshape(1, DF_WIDTH)
    usuf = jnp.asarray(_later_mask(TQ), dtype=BF16)
    tri = _later_mask(PAGE_SIZE).T
    trio = jnp.asarray(np.concatenate([tri, np.ones_like(tri)], axis=1), dtype=BF16)

    tm_p = TQ
    tiles_p = seq // tm_p
    xp2d = x_prompt.reshape(n_b * seq, D_MODEL)
    sh_a, sc_a, gt_a, sh_m, sc_m, gt_m = mod_p
    qt_p, kt_p, vt_p, kb_p, vtb_p, kn2_p = _qkv_prompt(xp2d, sc_a, sh_a, g_pre_attn[layer], w_in_b,
                                                       n_b, seq, tm_p)
    kb4 = kb_p.reshape(n_b, tiles_p, TQ, D_MODEL)
    mixed_sb = _sb_prompt_attention(qt_p, kb4, vtb_p, usuf)
    mixed_df = _diff_prompt_attention(qt_p, kb4, vtb_p, slope_pairs, sfeat, kfeat,
                                      _prefix_key_norms(kn2_p, n_b, tiles_p), lam_vecs, g_col)
    x1_p, h_p = _attn_out(mixed_sb.reshape(n_b * seq, SB_WIDTH), 0,
                          mixed_df.reshape(n_b * seq, DF_WIDTH), 0,
                          xp2d, w_out_b, g_post_attn[layer], gt_a, g_pre_mlp[layer], sc_m, sh_m,
                          tm_p, tiles_p)
    conv0 = jnp.zeros((n_b, CONV_WIDTH - 1, 2 * D_FF), F32)
    gate_p, conv_p = _up_conv(h_p.reshape(n_b, seq, D_MODEL), conv0, w_up_b,
                              conv_w[layer], conv_b[layer], tm_p)
    y_p = _down_proj(gate_p.reshape(n_b * seq, D_FF), x1_p, w_down_b, g_post_mlp[layer],
                     gt_m, tm_p, tiles_p)

    tm_s = n_dec * DEC_T
    xs2d = x_sample.reshape(tm_s, D_MODEL)
    sh_a, sc_a, gt_a, sh_m, sc_m, gt_m = mod_s
    q_s, k_s, v_s = _qkv_rows(xs2d, sc_a, sh_a, g_pre_attn[layer], w_in_b, tm_s)
    per_seq = lambda a: a.reshape(n_dec, DEC_T, D_MODEL)
    pages = lambda c: c[layer].transpose(0, 2, 3, 1).reshape(n_pool, D_MODEL, PAGE_SIZE)
    mixed_s = _dec_attention(
        page_table, per_seq(q_s), per_seq(k_s), per_seq(v_s), pages(cache_k), pages(cache_v),
        trio, slope_rep, lam_vecs, g8)
    mixed_s2d = mixed_s.reshape(tm_s, D_MODEL)
    x1_s, h_s = _attn_out(mixed_s2d, 0, mixed_s2d, 1, xs2d, w_out_b, g_post_attn[layer], gt_a,
                          g_pre_mlp[layer], sc_m, sh_m, tm_s, 1)
    prev_s = jnp.pad(state_conv[layer].astype(F32),
                     ((0, 0), (0, DEC_T - (CONV_WIDTH - 1)), (0, 0))).reshape(tm_s, 2 * D_FF)
    gate_s, ua_s, ub_s = _up_conv_rows(h_s, prev_s, w_up_b, conv_w[layer], conv_b[layer])
    conv_s = jnp.concatenate([ua_s, ub_s], axis=-1).reshape(n_dec, DEC_T, 2 * D_FF)[:, -2:]
    y_s = _down_proj(gate_s, x1_s, w_down_b, g_post_mlp[layer], gt_m, tm_s, 1)

    heads = lambda a, b, t: a.reshape(1, b, t, N_HEADS, HEAD_DIM)
    heads_t = lambda a: a.reshape(1, n_b, N_HEADS, HEAD_DIM, seq).transpose(0, 1, 4, 2, 3)
    return (y_p.reshape(n_b, seq, D_MODEL),
            y_s.reshape(n_dec, DEC_T, D_MODEL),
            heads_t(kt_p), heads_t(vt_p), conv_p[None],
            heads(k_s, n_dec, DEC_T), heads(v_s, n_dec, DEC_T), conv_s[None])
```

```python
import functools
import math

import jax
import jax.numpy as jnp
import ml_dtypes
import numpy as np
from jax import lax
from jax.experimental import pallas as pl
from jax.experimental.pallas import tpu as pltpu

F32 = jnp.float32
BF16 = jnp.bfloat16

D_MODEL = 1024
HEAD_DIM = 64
N_SB_HEADS = 8
N_DIFF_HEADS = 8
N_HEADS = N_SB_HEADS + N_DIFF_HEADS
SB_WIDTH = N_SB_HEADS * HEAD_DIM
DF_WIDTH = N_DIFF_HEADS * HEAD_DIM
DIFF_QK_DIM = HEAD_DIM // 2
D_FF = 2816
CONV_WIDTH = 3
PAGE_SIZE = 128
NORM_EPS = 1e-6
SUBLN_EPS = 1e-5
N_MOD = 6
LAMBDA_INIT = 0.8 - 0.6 * math.exp(-0.3 * 0)

LANES = 128
NEG_BIG = -1e30
LOG2E = math.log2(math.e)
DEAD_LOG2 = 104.0 * LOG2E + 1.0
BOUND_SLACK = 1.001

VMEM_LIMIT = 56 * 1024 * 1024
ROW_TILE = 512


def _dot(a, b):
    return jnp.dot(a, b, preferred_element_type=F32)


def _dot_nt(a, b):
    return lax.dot_general(a, b, (((1,), (1,)), ((), ())), preferred_element_type=F32)


def _rms(x, g, eps):
    return x * lax.rsqrt(jnp.mean(x * x, axis=-1, keepdims=True) + eps) * g


def _softplus2(z):
    return jnp.maximum(z, 0.0) + jnp.log2(1.0 + jnp.exp2(-jnp.abs(z)))


def _split_bf16(x):
    hi = x.astype(BF16)
    lo = (x - hi.astype(F32)).astype(BF16)
    return hi, lo


def _mod_kernel(c_ref, w_ref, b_ref, o_ref):
    c = c_ref[...]
    s = c * (1.0 / (1.0 + jnp.exp(-c)))
    o_ref[...] = _dot(s.astype(BF16), w_ref[...].astype(BF16)) + b_ref[...]


def _modulation(c_all, w_ada, b_ada):
    n_rows = c_all.shape[0]
    n_out = w_ada.shape[1]
    tn = 1536
    return pl.pallas_call(
        _mod_kernel,
        out_shape=jax.ShapeDtypeStruct((n_rows, n_out), F32),
        grid=(n_out // tn,),
        in_specs=[
            pl.BlockSpec((n_rows, D_MODEL), lambda j: (0, 0)),
            pl.BlockSpec((D_MODEL, tn), lambda j: (0, j)),
            pl.BlockSpec((1, tn), lambda j: (0, j)),
        ],
        out_specs=pl.BlockSpec((n_rows, tn), lambda j: (0, j)),
        compiler_params=pltpu.CompilerParams(
            dimension_semantics=("arbitrary",), vmem_limit_bytes=VMEM_LIMIT),
        name="adaln_mod",
    )(c_all, w_ada, b_ada.reshape(1, n_out))


def _scaled_qkv(x_ref, sc_ref, sh_ref, g_ref, w_ref):
    x = x_ref[...]
    h = _rms(x, g_ref[...], NORM_EPS) * (1.0 + sc_ref[...]) + sh_ref[...]
    qkv = _dot(h.astype(BF16), w_ref[...])
    lane = lax.broadcasted_iota(jnp.int32, (1, D_MODEL), 1)
    qscale = jnp.where(lane < SB_WIDTH, LOG2E * HEAD_DIM ** -0.5,
                       LOG2E * DIFF_QK_DIM ** -0.5).astype(F32)
    return qkv[:, :D_MODEL] * qscale, qkv[:, D_MODEL:2 * D_MODEL], qkv[:, 2 * D_MODEL:]


def _qkv_rows_kernel(x_ref, sc_ref, sh_ref, g_ref, w_ref, q_ref, k_ref, v_ref):
    q, k, v = _scaled_qkv(x_ref, sc_ref, sh_ref, g_ref, w_ref)
    q_ref[...] = q.astype(BF16)
    k_ref[...] = k
    v_ref[...] = v


def _qkv_prompt_kernel(x_ref, sc_ref, sh_ref, g_ref, w_ref, ind_ref,
                       qt_ref, kt_ref, vt_ref, kb_ref, vtb_ref, kn_ref):
    q, k, v = _scaled_qkv(x_ref, sc_ref, sh_ref, g_ref, w_ref)
    qt_ref[...] = q.T.astype(BF16)
    kt_ref[...] = k.T
    vt = v.T
    vt_ref[...] = vt
    vtb_ref[...] = vt.astype(BF16)
    kb = k.astype(BF16)
    kb_ref[...] = kb
    kf = kb.astype(F32)
    hi, lo = _split_bf16(kf * kf)
    ind = ind_ref[...]
    norms2 = _dot(hi, ind) + _dot(lo, ind)
    kn_ref[...] = jnp.max(norms2, axis=0, keepdims=True) * (1.0 + 2.0 ** -12)


def _mod_spec(mod_rows, tiles_per_group):
    return pl.BlockSpec((None, mod_rows, D_MODEL), lambda i: (i // tiles_per_group, 0, 0))


def _qkv_in_specs(sc, sh, tm, tiles_per_group):
    const2 = lambda i: (0, 0)
    return [
        pl.BlockSpec((tm, D_MODEL), lambda i: (i, 0)),
        _mod_spec(sc.shape[1], tiles_per_group),
        _mod_spec(sh.shape[1], tiles_per_group),
        pl.BlockSpec((1, D_MODEL), const2),
        pl.BlockSpec((D_MODEL, 3 * D_MODEL), const2),
    ]


def _qkv_rows(x2d, sc, sh, g, w_bf16, tm):
    n = x2d.shape[0]
    row_spec = pl.BlockSpec((tm, D_MODEL), lambda i: (i, 0))
    out_f32 = jax.ShapeDtypeStruct((n, D_MODEL), F32)
    return pl.pallas_call(
        _qkv_rows_kernel,
        out_shape=(jax.ShapeDtypeStruct((n, D_MODEL), BF16), out_f32, out_f32),
        grid=(n // tm,),
        in_specs=_qkv_in_specs(sc, sh, tm, 1),
        out_specs=(row_spec, row_spec, row_spec),
        compiler_params=pltpu.CompilerParams(
            dimension_semantics=("arbitrary",), vmem_limit_bytes=VMEM_LIMIT),
        name="qkv_rows",
    )(x2d, sc, sh, g.reshape(1, D_MODEL), w_bf16)


def _qkv_prompt(x2d, sc, sh, g, w_bf16, n_b, seq, tm):
    tiles = seq // tm
    t_spec = pl.BlockSpec((None, D_MODEL, tm), lambda i: (i // tiles, 0, i % tiles))
    t_f32 = jax.ShapeDtypeStruct((n_b, D_MODEL, seq), F32)
    group_of_lane = np.arange(D_MODEL)[:, None] // DIFF_QK_DIM == np.arange(LANES)[None, :]
    ind = jnp.asarray(group_of_lane.astype(np.float32), dtype=BF16)
    return pl.pallas_call(
        _qkv_prompt_kernel,
        out_shape=(jax.ShapeDtypeStruct((n_b, D_MODEL, seq), BF16), t_f32, t_f32,
                   jax.ShapeDtypeStruct((n_b * seq, D_MODEL), BF16),
                   jax.ShapeDtypeStruct((n_b, tiles, D_MODEL, tm), BF16),
                   jax.ShapeDtypeStruct((n_b * tiles, 1, LANES), F32)),
        grid=(n_b * tiles,),
        in_specs=_qkv_in_specs(sc, sh, tm, tiles) + [pl.BlockSpec((D_MODEL, LANES), lambda i: (0, 0))],
        out_specs=(t_spec, t_spec, t_spec,
                   pl.BlockSpec((tm, D_MODEL), lambda i: (i, 0)),
                   pl.BlockSpec((None, None, D_MODEL, tm), lambda i: (i // tiles, i % tiles, 0, 0)),
                   pl.BlockSpec((None, 1, LANES), lambda i: (i, 0, 0))),
        compiler_params=pltpu.CompilerParams(
            dimension_semantics=("arbitrary",), vmem_limit_bytes=VMEM_LIMIT),
        name="qkv_prompt",
    )(x2d, sc, sh, g.reshape(1, D_MODEL), w_bf16, ind)


def _prefix_key_norms(kn2, n_b, tiles):
    n_groups = 2 * N_DIFF_HEADS
    first = SB_WIDTH // DIFF_QK_DIM
    kn = jnp.sqrt(kn2.reshape(n_b, tiles, LANES)[:, :, first:first + n_groups])
    kn = lax.cummax(kn, axis=1)
    kn = kn.reshape(n_b, tiles, N_DIFF_HEADS // 2, 4).transpose(0, 2, 1, 3)
    return jnp.pad(kn, ((0, 0), (0, 0), (0, 0), (0, LANES - 4)))


def _lambda_value(lam_ref):
    lq1 = lam_ref[0:1, :]
    lk1 = lam_ref[1:2, :]
    lq2 = lam_ref[2:3, :]
    lk2 = lam_ref[3:4, :]
    return (jnp.exp(jnp.sum(lq1 * lk1, axis=1, keepdims=True))
            - jnp.exp(jnp.sum(lq2 * lk2, axis=1, keepdims=True)) + LAMBDA_INIT)


TQ = 256
SB_STEP_HEADS = 4
SB_STEP_WIDTH = SB_STEP_HEADS * HEAD_DIM
ONES_ROWS = 16
DIFF_TRIPS = ((1, 1), (2, 1), (4, 1), (8, 1 << 30), (4, 1), (2, 1), (1, 1))
DIFF_LEAD_TIERS = 3
DIFF_STEEP_PAIRS = 2


def _prompt_specs(s, first_group, width):
    n_blk = s // TQ
    return [
        pl.BlockSpec((None, width, TQ), lambda bi, p, i: (bi, first_group + p, i)),
        pl.BlockSpec((None, n_blk, TQ, width), lambda bi, p, i: (bi, 0, 0, first_group + p)),
        pl.BlockSpec((None, n_blk, width, TQ), lambda bi, p, i: (bi, 0, first_group + p, 0)),
    ]


def _sb_prompt_kernel(qt_ref, k_ref, vt_ref, usuf_ref, o_ref, acc_ref):
    i = pl.program_id(2)
    heads = range(SB_STEP_HEADS)
    qt = qt_ref[...].astype(F32)
    drow = lax.broadcasted_iota(jnp.int32, (SB_STEP_WIDTH, TQ), 0)
    key = lax.broadcasted_iota(jnp.int32, (TQ, TQ), 0)
    qry = lax.broadcasted_iota(jnp.int32, (TQ, TQ), 1)
    strictly_causal = key < qry
    usuf = usuf_ref[...]
    qtm = [jnp.where((drow >= HEAD_DIM * h) & (drow < HEAD_DIM * (h + 1)), qt, 0.0).astype(BF16)
           for h in heads]

    def step(js, runs, masked_first):
        streams = [(b, h) for b in range(len(js)) for h in heads]
        kbs = [k_ref[j] for j in js]
        vts = [vt_ref[j] for j in js]
        zs = {(b, h): _dot(kbs[b], qtm[h]) for b, h in streams}
        log_sig, parts, offsets = {}, {}, {}
        runs = list(runs)
        for b, h in streams:
            masked = masked_first and b == 0
            sp = _softplus2(zs[b, h])
            log_sig[b, h] = zs[b, h] - sp
            if masked:
                sp = jnp.where(strictly_causal, sp, 0.0)
            parts[b, h] = _split_bf16(sp)
            offsets[b, h] = runs[h]
            runs[h] = runs[h] + jnp.sum(sp, axis=0, keepdims=True)
        between = {s: _dot(usuf, parts[s][0]) + _dot(usuf, parts[s][1]) for s in streams}
        weights = {}
        for b, h in streams:
            w = jnp.exp2(log_sig[b, h] - between[b, h] - offsets[b, h])
            if masked_first and b == 0:
                w = jnp.where(strictly_causal, w, 0.0)
            weights[b, h] = w.astype(BF16)
        outs = {(b, h): _dot(vts[b][HEAD_DIM * h:HEAD_DIM * (h + 1)], weights[b, h])
                for b, h in streams}
        for h in heads:
            acc = acc_ref[h]
            for b in range(len(js)):
                acc = acc + outs[b, h]
            acc_ref[h] = acc
        return tuple(runs)

    def alive(runs):
        least = runs[0]
        for r in runs[1:]:
            least = jnp.minimum(least, r)
        return (jnp.min(least) < DEAD_LOG2).astype(jnp.int32)

    acc_ref[...] = jnp.zeros_like(acc_ref)
    zeros = (jnp.zeros((1, TQ), F32),) * SB_STEP_HEADS
    runs = lax.cond(i > 0,
                    lambda: step([i, i - 1], zeros, True),
                    lambda: step([i], zeros, True))

    def cond(c):
        return (c[0] < i) & (c[1] > 0)

    def body(c):
        runs = step([i - 1 - c[0]], c[2:], False)
        return (c[0] + 1, alive(runs)) + runs

    lax.while_loop(cond, body, (jnp.int32(1), alive(runs)) + runs)
    out_t = jnp.concatenate([acc_ref[h] for h in heads], axis=0)
    o_ref[...] = out_t.T.astype(BF16)


def _sb_prompt_attention(qt, kb4, vtb4, usuf):
    b, _, s = qt.shape
    return pl.pallas_call(
        _sb_prompt_kernel,
        out_shape=jax.ShapeDtypeStruct((b, s, SB_WIDTH), BF16),
        grid=(b, SB_WIDTH // SB_STEP_WIDTH, s // TQ),
        in_specs=_prompt_specs(s, 0, SB_STEP_WIDTH)
        + [pl.BlockSpec((TQ, TQ), lambda bi, p, i: (0, 0))],
        out_specs=pl.BlockSpec((None, TQ, SB_STEP_WIDTH), lambda bi, p, i: (bi, i, p)),
        scratch_shapes=[pltpu.VMEM((SB_STEP_HEADS, HEAD_DIM, TQ), F32)],
        compiler_params=pltpu.CompilerParams(
            dimension_semantics=("arbitrary", "arbitrary", "arbitrary"),
            vmem_limit_bytes=VMEM_LIMIT),
        name="sb_prompt_attn",
    )(qt, kb4, vtb4, usuf)


def _diff_prompt_kernel(qt_ref, k_ref, vt_ref, slope_ref, sfeat_ref, kfeat_ref, kmax_ref, lam_ref,
                        g_ref, o_ref, acc_ref):
    i = pl.program_id(2)
    qt = qt_ref[...].astype(F32)
    drow = lax.broadcasted_iota(jnp.int32, (LANES, TQ), 0)
    key = lax.broadcasted_iota(jnp.int32, (TQ, TQ), 0)
    qry = lax.broadcasted_iota(jnp.int32, (TQ, TQ), 1)
    causal = key <= qry
    lam = _lambda_value(lam_ref)
    slopes = [slope_ref[:, HEAD_DIM * h:HEAD_DIM * h + 1] for h in range(2)]
    kfeat = kfeat_ref[...]
    qext = []
    for h in range(2):
        for c in range(2):
            lo_row = HEAD_DIM * h + DIFF_QK_DIM * c
            in_map = (drow >= lo_row) & (drow < lo_row + DIFF_QK_DIM)
            qtm = jnp.where(in_map, qt, 0.0).astype(BF16)
            qext.append(jnp.concatenate([qtm, sfeat_ref[h]], axis=0))

    ones_rows = jnp.ones((ONES_ROWS, TQ), BF16)

    def step(js, ms, masked):
        ms = list(ms)
        kexts = [jnp.concatenate([k_ref[j], kfeat], axis=1) for j in js]
        scores = [[_dot(kext, qext[idx]) for idx in range(4)] for kext in kexts]
        pending = []
        for b, j in enumerate(js):
            alphas, probs = [], []
            for idx in range(4):
                h = idx // 2
                s = scores[b][idx]
                if masked:
                    s = jnp.where(causal, s, NEG_BIG)
                shift = slopes[h] * ((i - j) * TQ).astype(F32)
                m_new = jnp.maximum(ms[idx], jnp.max(s, axis=0, keepdims=True) - shift)
                alphas.append(jnp.exp2(ms[idx] - m_new))
                probs.append(jnp.exp2(s - (m_new + shift)).astype(BF16))
                ms[idx] = m_new
            vt = vt_ref[j]
            vts = [jnp.concatenate([vt[HEAD_DIM * h:HEAD_DIM * (h + 1)], ones_rows], axis=0)
                   for h in range(2)]
            pending.append((alphas, [_dot(vts[idx // 2], probs[idx]) for idx in range(4)]))
        for idx in range(4):
            acc = acc_ref[idx]
            for alphas, outs in pending:
                acc = alphas[idx] * acc + outs[idx]
            acc_ref[idx] = acc
        return tuple(ms)

    q_norm = []
    for idx in range(4):
        lo_row = DIFF_QK_DIM * idx
        in_map = (drow >= lo_row) & (drow < lo_row + DIFF_QK_DIM)
        q_norm.append(jnp.sqrt(jnp.sum(jnp.where(in_map, qt * qt, 0.0), axis=0, keepdims=True)))

    def alive(rem, ms):
        k_norm = kmax_ref[pl.ds(jnp.maximum(rem - 1, 0), 1), :]
        gap = ((TQ - 1) - (i - rem + 1) * TQ).astype(F32)
        worst = jnp.full((1, TQ), NEG_BIG, F32)
        for idx in range(4):
            bound = q_norm[idx] * k_norm[:, idx:idx + 1] * BOUND_SLACK + slopes[idx // 2] * gap
            worst = jnp.maximum(worst, bound - ms[idx])
        return (jnp.max(worst) > -DEAD_LOG2).astype(jnp.int32)

    acc_ref[...] = jnp.zeros_like(acc_ref)
    ms = step([i], (jnp.full((1, TQ), NEG_BIG, F32),) * 4, True)

    state = (i, alive(i, ms)) + ms
    steep = (pl.program_id(1) < DIFF_STEEP_PAIRS).astype(jnp.int32)
    for tier, (group, max_trips) in enumerate(DIFF_TRIPS):
        if tier < DIFF_LEAD_TIERS:
            max_trips = max_trips * steep

        def cond(c, group=group, max_trips=max_trips):
            return (c[0] < max_trips) & (c[1] >= group) & (c[2] > 0)

        def body(c, group=group):
            rem, ms = c[1], c[3:]
            still = alive(rem - group, ms)
            ms = step([rem - group + u for u in range(group)], ms, False)
            return (c[0] + 1, rem - group, still) + ms

        state = lax.while_loop(cond, body, (jnp.int32(0),) + state)[1:]

    y_t = []
    for h in range(2):
        maps = [acc_ref[2 * h + c, :HEAD_DIM, :] / acc_ref[2 * h + c, HEAD_DIM:HEAD_DIM + 1, :]
                for c in range(2)]
        o_t = maps[0] - lam * maps[1]
        ms_t = jnp.mean(o_t * o_t, axis=0, keepdims=True)
        y_t.append(o_t * lax.rsqrt(ms_t + SUBLN_EPS) * g_ref[...] * (1.0 - LAMBDA_INIT))
    o_ref[...] = jnp.concatenate(y_t, axis=0).T.astype(BF16)


def _diff_prompt_attention(qt, kb4, vtb4, slopes, sfeat, kfeat, kmax, lam_vecs, g_col):
    b, _, s = qt.shape
    n_pairs = DF_WIDTH // LANES
    const2 = lambda bi, p, i: (0, 0)
    return pl.pallas_call(
        _diff_prompt_kernel,
        out_shape=jax.ShapeDtypeStruct((b, s, DF_WIDTH), BF16),
        grid=(b, n_pairs, s // TQ),
        in_specs=_prompt_specs(s, SB_WIDTH // LANES, LANES) + [
            pl.BlockSpec((None, 1, LANES), lambda bi, p, i: (p, 0, 0)),
            pl.BlockSpec((None, 2, LANES, TQ), lambda bi, p, i: (p, 0, 0, 0)),
            pl.BlockSpec((TQ, LANES), const2),
            pl.BlockSpec((None, None, s // TQ, LANES), lambda bi, p, i: (bi, p, 0, 0)),
            pl.BlockSpec((4, DIFF_QK_DIM), const2),
            pl.BlockSpec((HEAD_DIM, 1), const2),
        ],
        out_specs=pl.BlockSpec((None, TQ, LANES), lambda bi, p, i: (bi, i, p)),
        scratch_shapes=[pltpu.VMEM((4, HEAD_DIM + ONES_ROWS, TQ), F32)],
        compiler_params=pltpu.CompilerParams(
            dimension_semantics=("arbitrary", "arbitrary", "arbitrary"),
            vmem_limit_bytes=VMEM_LIMIT),
        name="diff_prompt_attn",
    )(qt, kb4, vtb4, slopes, sfeat, kfeat, kmax, lam_vecs, g_col)


PAGES_PER_STEP = 16
DEC_T = 8
SB_ROWS = N_SB_HEADS * DEC_T
DF_ROWS = N_DIFF_HEADS * 2 * DEC_T


def _group_rms_lanes(o, g, lane, n_groups):
    sq = o * o
    ms = jnp.zeros_like(o)
    for h in range(n_groups):
        in_h = (lane >= HEAD_DIM * h) & (lane < HEAD_DIM * (h + 1))
        s_h = jnp.sum(jnp.where(in_h, sq, 0.0), axis=1, keepdims=True)
        ms = jnp.where(in_h, s_h * (1.0 / HEAD_DIM), ms)
    return o * lax.rsqrt(ms + SUBLN_EPS) * g


def _dec_attn_kernel(pt_ref, q_ref, kn_ref, vn_ref, *rest, past_len):
    del pt_ref
    kp_refs = rest[:PAGES_PER_STEP]
    vp_refs = rest[PAGES_PER_STEP:2 * PAGES_PER_STEP]
    (trio_ref, slope_ref, lam_ref, g_ref, o_ref, wsb_ref, wdf_ref,
     run_ref, accsb_ref, mcol_ref, mrep_ref, l_ref, accdf_ref) = rest[2 * PAGES_PER_STEP:]
    step_id = pl.program_id(1)
    n_steps = pl.num_programs(1)

    key_lane = lax.broadcasted_iota(jnp.int32, (DF_ROWS, PAGE_SIZE), 1)
    t_df = lax.broadcasted_iota(jnp.int32, (DF_ROWS, PAGE_SIZE), 0) & (DEC_T - 1)
    slope = slope_ref[...]
    bias0 = slope * (key_lane - past_len - t_df).astype(F32)
    trio = trio_ref[...]

    def sb_group(chunks):
        offset = run_ref[...]
        acc = accsb_ref[...]
        parts = []
        for z, _, mask in chunks:
            sp = _softplus2(z)
            log_sig = z - sp
            if mask is not None:
                sp = jnp.where(mask, sp, 0.0)
            parts.append((log_sig, _split_bf16(sp)))
        sums = [_dot(hi, trio) + _dot(lo, trio) for _, (hi, lo) in parts]
        for (_, pv, mask), (log_sig, _), sm in zip(chunks, parts, sums):
            w = jnp.exp2(log_sig - sm[:, :PAGE_SIZE] - offset)
            if mask is not None:
                w = jnp.where(mask, w, 0.0)
            acc = acc + pv(w.astype(BF16))
            offset = offset + sm[:, PAGE_SIZE:]
        run_ref[...] = offset
        accsb_ref[...] = acc

    def df_group(chunks):
        top = chunks[0][0]
        for s, _ in chunks[1:]:
            top = jnp.maximum(top, s)
        m_old = mcol_ref[...]
        m_new = jnp.maximum(m_old, jnp.max(top, axis=1, keepdims=True))
        m_rep_old = mrep_ref[...]
        m_rep = jnp.broadcast_to(m_new, (DF_ROWS, PAGE_SIZE))
        p_sum = jnp.zeros((DF_ROWS, PAGE_SIZE), F32)
        pv_sum = jnp.zeros((DF_ROWS, DF_WIDTH), F32)
        for s, pv in chunks:
            p = jnp.exp2(s - m_rep)
            p_sum = p_sum + p
            pv_sum = pv_sum + pv(p.astype(BF16))
        alpha_rep = jnp.exp2(m_rep_old - m_rep)
        l_ref[...] = jnp.exp2(m_old - m_new) * l_ref[...] + jnp.sum(p_sum, axis=1, keepdims=True)
        accdf_ref[...] = jnp.concatenate([alpha_rep] * (DF_WIDTH // PAGE_SIZE), axis=1) * accdf_ref[...] + pv_sum
        mcol_ref[...] = m_new
        mrep_ref[...] = m_rep

    @pl.when(step_id == 0)
    def _init():
        qf = q_ref[...].astype(F32)
        q_sb = jnp.concatenate([qf[:, :SB_WIDTH]] * N_SB_HEADS, axis=0)
        r = lax.broadcasted_iota(jnp.int32, (SB_ROWS, SB_WIDTH), 0)
        c = lax.broadcasted_iota(jnp.int32, (SB_ROWS, SB_WIDTH), 1)
        wsb_ref[...] = jnp.where((r >> 3) == (c >> 6), q_sb, 0.0).astype(BF16)
        q_df = jnp.concatenate([qf[:, SB_WIDTH:]] * (2 * N_DIFF_HEADS), axis=0)
        r = lax.broadcasted_iota(jnp.int32, (DF_ROWS, DF_WIDTH), 0)
        c = lax.broadcasted_iota(jnp.int32, (DF_ROWS, DF_WIDTH), 1)
        wdf_ref[...] = jnp.where((r >> 3) == (c >> 5), q_df, 0.0).astype(BF16)
        run_ref[...] = jnp.zeros_like(run_ref)
        accsb_ref[...] = jnp.zeros_like(accsb_ref)
        mcol_ref[...] = jnp.full_like(mcol_ref, NEG_BIG)
        mrep_ref[...] = jnp.full_like(mrep_ref, NEG_BIG)
        l_ref[...] = jnp.zeros_like(l_ref)
        accdf_ref[...] = jnp.zeros_like(accdf_ref)
        pad = jnp.zeros((PAGE_SIZE - DEC_T, D_MODEL), F32)
        kn = jnp.concatenate([kn_ref[...], pad], axis=0).astype(BF16)
        vn = jnp.concatenate([vn_ref[...], pad], axis=0).astype(BF16)
        key_sb = lax.broadcasted_iota(jnp.int32, (SB_ROWS, PAGE_SIZE), 1)
        t_sb = lax.broadcasted_iota(jnp.int32, (SB_ROWS, PAGE_SIZE), 0) & (DEC_T - 1)
        sb_group([(_dot_nt(wsb_ref[...], kn[:, :SB_WIDTH]),
                   lambda w: _dot(w, vn[:, :SB_WIDTH]), key_sb < t_sb)])
        s_new = _dot_nt(wdf_ref[...], kn[:, SB_WIDTH:]) + (bias0 + slope * float(past_len))
        df_group([(jnp.where(key_lane <= t_df, s_new, NEG_BIG), lambda p: _dot(p, vn[:, SB_WIDTH:]))])

    n_pages = n_steps * PAGES_PER_STEP
    wsb = wsb_ref[...]
    wdf = wdf_ref[...]
    sb_chunks, df_chunks = [], []
    for r in range(PAGES_PER_STEP):
        page = n_pages - 1 - (step_id * PAGES_PER_STEP + r)
        kp = kp_refs[r][...].astype(BF16)
        vp = vp_refs[r][...].astype(BF16)
        sb_chunks.append((_dot(wsb, kp[:SB_WIDTH, :]),
                          functools.partial(_dot_nt, b=vp[:SB_WIDTH, :]), None))
        base = (page * PAGE_SIZE).astype(F32)
        df_chunks.append((_dot(wdf, kp[SB_WIDTH:, :]) + (bias0 + slope * base),
                          functools.partial(_dot_nt, b=vp[SB_WIDTH:, :])))
    sb_group(sb_chunks)
    df_group(df_chunks)

    @pl.when(step_id == n_steps - 1)
    def _finish():
        lane = lax.broadcasted_iota(jnp.int32, (DEC_T, SB_WIDTH), 1)
        lam = _lambda_value(lam_ref)
        acc_sb = accsb_ref[...]
        acc_df = accdf_ref[...] / l_ref[...]
        o_sb = jnp.zeros((DEC_T, SB_WIDTH), F32)
        o_df = jnp.zeros((DEC_T, DF_WIDTH), F32)
        for h in range(N_SB_HEADS):
            in_h = (lane >> 6) == h
            o_sb = jnp.where(in_h, acc_sb[DEC_T * h:DEC_T * (h + 1), :], o_sb)
            r0 = 2 * DEC_T * h
            o_h = acc_df[r0:r0 + DEC_T, :] - lam * acc_df[r0 + DEC_T:r0 + 2 * DEC_T, :]
            o_df = jnp.where(in_h, o_h, o_df)
        y_df = _group_rms_lanes(o_df, g_ref[...], lane, N_DIFF_HEADS) * (1.0 - LAMBDA_INIT)
        o_ref[:, :SB_WIDTH] = o_sb.astype(BF16)
        o_ref[:, SB_WIDTH:] = y_df.astype(BF16)


def _dec_attention(page_table, q, k_new, v_new, cache_k, cache_v, trio, slope_rep, lam_vecs, g8):
    n_seq, n_pages = page_table.shape
    past_len = n_pages * PAGE_SIZE
    n_steps = n_pages // PAGES_PER_STEP

    def page_spec(r):
        def idx(b, s, pt):
            return (pt[b, n_pages - 1 - (s * PAGES_PER_STEP + r)], 0, 0)
        return pl.BlockSpec((None, D_MODEL, PAGE_SIZE), idx)

    seq_spec = pl.BlockSpec((None, DEC_T, D_MODEL), lambda b, s, pt: (b, 0, 0))
    const2 = lambda b, s, pt: (0, 0)
    grid_spec = pltpu.PrefetchScalarGridSpec(
        num_scalar_prefetch=1,
        grid=(n_seq, n_steps),
        in_specs=[seq_spec, seq_spec, seq_spec]
        + [page_spec(r) for r in range(PAGES_PER_STEP)]
        + [page_spec(r) for r in range(PAGES_PER_STEP)]
        + [
            pl.BlockSpec((PAGE_SIZE, 2 * PAGE_SIZE), const2),
            pl.BlockSpec((DF_ROWS, PAGE_SIZE), const2),
            pl.BlockSpec((4, DIFF_QK_DIM), const2),
            pl.BlockSpec((1, DF_WIDTH), const2),
        ],
        out_specs=seq_spec,
        scratch_shapes=[
            pltpu.VMEM((SB_ROWS, SB_WIDTH), BF16),
            pltpu.VMEM((DF_ROWS, DF_WIDTH), BF16),
            pltpu.VMEM((SB_ROWS, PAGE_SIZE), F32),
            pltpu.VMEM((SB_ROWS, SB_WIDTH), F32),
            pltpu.VMEM((DF_ROWS, 1), F32),
            pltpu.VMEM((DF_ROWS, PAGE_SIZE), F32),
            pltpu.VMEM((DF_ROWS, 1), F32),
            pltpu.VMEM((DF_ROWS, DF_WIDTH), F32),
        ],
    )
    return pl.pallas_call(
        functools.partial(_dec_attn_kernel, past_len=past_len),
        out_shape=jax.ShapeDtypeStruct((n_seq, DEC_T, D_MODEL), BF16),
        grid_spec=grid_spec,
        compiler_params=pltpu.CompilerParams(
            dimension_semantics=("arbitrary", "arbitrary"), vmem_limit_bytes=VMEM_LIMIT),
        name="dec_attn",
    )(page_table, q, k_new, v_new, *([cache_k] * PAGES_PER_STEP), *([cache_v] * PAGES_PER_STEP),
      trio, slope_rep, lam_vecs, g8)


def _attn_out_kernel(ma_ref, mb_ref, x_ref, wa_ref, wb_ref, gpost_ref, gt_ref,
                     gpre_ref, sc_ref, sh_ref, x1_ref, h_ref):
    y = _dot(ma_ref[...], wa_ref[...]) + _dot(mb_ref[...], wb_ref[...])
    x1 = x_ref[...] + gt_ref[...] * _rms(y, gpost_ref[...], NORM_EPS)
    x1_ref[...] = x1
    h = _rms(x1, gpre_ref[...], NORM_EPS) * (1.0 + sc_ref[...]) + sh_ref[...]
    h_ref[...] = h.astype(BF16)


def _attn_out(mixed_a, a_col, mixed_b, b_col, x2d, w_out_bf16, g_post, gt, g_pre, sc, sh,
              tm, tiles_per_group):
    n = x2d.shape[0]
    half = D_MODEL // 2
    row_spec = pl.BlockSpec((tm, D_MODEL), lambda i: (i, 0))
    const2 = lambda i: (0, 0)
    vec_spec = pl.BlockSpec((1, D_MODEL), const2)
    mod = lambda a: _mod_spec(a.shape[1], tiles_per_group)
    return pl.pallas_call(
        _attn_out_kernel,
        out_shape=(jax.ShapeDtypeStruct((n, D_MODEL), F32),
                   jax.ShapeDtypeStruct((n, D_MODEL), BF16)),
        grid=(n // tm,),
        in_specs=[
            pl.BlockSpec((tm, half), lambda i: (i, a_col)),
            pl.BlockSpec((tm, half), lambda i: (i, b_col)),
            row_spec,
            pl.BlockSpec((half, D_MODEL), lambda i: (0, 0)),
            pl.BlockSpec((half, D_MODEL), lambda i: (1, 0)),
            vec_spec, mod(gt), vec_spec, mod(sc), mod(sh),
        ],
        out_specs=(row_spec, row_spec),
        compiler_params=pltpu.CompilerParams(
            dimension_semantics=("arbitrary",), vmem_limit_bytes=VMEM_LIMIT),
        name="attn_out",
    )(mixed_a, mixed_b, x2d, w_out_bf16, w_out_bf16, g_post.reshape(1, D_MODEL), gt,
      g_pre.reshape(1, D_MODEL), sc, sh)


UP_CHUNK = 1408


def _up_conv_kernel(h_ref, prev_ref, wup_ref, cw_ref, cb_ref, g_ref, st_ref, carry_ref, *, tm):
    t = pl.program_id(1)

    @pl.when(t == 0)
    def _load_state():
        carry_ref[...] = prev_ref[...]

    h = h_ref[...]
    row = lax.broadcasted_iota(jnp.int32, (tm, 1), 0)

    def conv_cols(c0):
        u = _dot(h, wup_ref[:, c0:c0 + UP_CHUNK])
        p0 = carry_ref[0:1, c0:c0 + UP_CHUNK]
        p1 = carry_ref[1:2, c0:c0 + UP_CHUNK]
        u1 = jnp.where(row == 0, p1, pltpu.roll(u, 1, 0))
        u2 = jnp.where(row == 0, p0, jnp.where(row == 1, p1, pltpu.roll(u, 2, 0)))
        w0 = cw_ref[0:1, c0:c0 + UP_CHUNK]
        w1 = cw_ref[1:2, c0:c0 + UP_CHUNK]
        w2 = cw_ref[2:3, c0:c0 + UP_CHUNK]
        conv = cb_ref[:, c0:c0 + UP_CHUNK] + w0 * u2 + w1 * u1 + w2 * u
        last = u[tm - 2:tm, :]
        carry_ref[:, c0:c0 + UP_CHUNK] = last
        st_ref[:, c0:c0 + UP_CHUNK] = last
        return conv

    for ch in range(D_FF // UP_CHUNK):
        a = conv_cols(ch * UP_CHUNK)
        b = conv_cols(D_FF + ch * UP_CHUNK)
        gate = a * (1.0 / (1.0 + jnp.exp(-a))) * b
        g_ref[:, ch * UP_CHUNK:(ch + 1) * UP_CHUNK] = gate.astype(BF16)


def _up_conv(h3d, conv_prev, w_up_bf16, conv_w, conv_b, tm):
    nb, t_len, _ = h3d.shape
    const2 = lambda b, t: (0, 0)
    return pl.pallas_call(
        functools.partial(_up_conv_kernel, tm=tm),
        out_shape=(jax.ShapeDtypeStruct((nb, t_len, D_FF), BF16),
                   jax.ShapeDtypeStruct((nb, CONV_WIDTH - 1, 2 * D_FF), F32)),
        grid=(nb, t_len // tm),
        in_specs=[
            pl.BlockSpec((None, tm, D_MODEL), lambda b, t: (b, t, 0)),
            pl.BlockSpec((None, CONV_WIDTH - 1, 2 * D_FF), lambda b, t: (b, 0, 0)),
            pl.BlockSpec((D_MODEL, 2 * D_FF), const2),
            pl.BlockSpec((CONV_WIDTH, 2 * D_FF), const2),
            pl.BlockSpec((1, 2 * D_FF), const2),
        ],
        out_specs=(pl.BlockSpec((None, tm, D_FF), lambda b, t: (b, t, 0)),
                   pl.BlockSpec((None, CONV_WIDTH - 1, 2 * D_FF), lambda b, t: (b, 0, 0))),
        scratch_shapes=[pltpu.VMEM((CONV_WIDTH - 1, 2 * D_FF), F32)],
        compiler_params=pltpu.CompilerParams(
            dimension_semantics=("arbitrary", "arbitrary"), vmem_limit_bytes=VMEM_LIMIT),
        name="up_conv_gate",
    )(h3d, conv_prev, w_up_bf16, conv_w, conv_b.reshape(1, 2 * D_FF))


def _up_conv_rows_kernel(h_ref, pa_ref, pb_ref, wa_ref, wb_ref, cwa_ref, cwb_ref, cba_ref, cbb_ref,
                         g_ref, ua_ref, ub_ref):
    h = h_ref[...]
    rows = h.shape[0]
    t = lax.broadcasted_iota(jnp.int32, (rows, 1), 0) & (DEC_T - 1)

    def conv(w_ref, prev_ref, cw_ref, cb_ref, u_ref):
        u = _dot(h, w_ref[...])
        u_ref[...] = u
        prev = prev_ref[...]
        u1 = jnp.where(t == 0, pltpu.roll(prev, rows - 1, 0), pltpu.roll(u, 1, 0))
        u2 = jnp.where(t < 2, prev, pltpu.roll(u, 2, 0))
        return cb_ref[...] + cw_ref[0:1, :] * u2 + cw_ref[1:2, :] * u1 + cw_ref[2:3, :] * u

    a = conv(wa_ref, pa_ref, cwa_ref, cba_ref, ua_ref)
    b = conv(wb_ref, pb_ref, cwb_ref, cbb_ref, ub_ref)
    g_ref[...] = (a * (1.0 / (1.0 + jnp.exp(-a))) * b).astype(BF16)


def _up_conv_rows(h2d, prev2d, w_up_bf16, conv_w, conv_b):
    rows = h2d.shape[0]
    n_chunks = D_FF // UP_CHUNK
    half_a = lambda r: pl.BlockSpec((r, UP_CHUNK), lambda c: (0, c))
    half_b = lambda r: pl.BlockSpec((r, UP_CHUNK), lambda c: (0, n_chunks + c))
    u_half = jax.ShapeDtypeStruct((rows, D_FF), F32)
    cb2d = conv_b.reshape(1, 2 * D_FF)
    return pl.pallas_call(
        _up_conv_rows_kernel,
        out_shape=(jax.ShapeDtypeStruct((rows, D_FF), BF16), u_half, u_half),
        grid=(n_chunks,),
        in_specs=[pl.BlockSpec((rows, D_MODEL), lambda c: (0, 0)),
                  half_a(rows), half_b(rows), half_a(D_MODEL), half_b(D_MODEL),
                  half_a(CONV_WIDTH), half_b(CONV_WIDTH), half_a(1), half_b(1)],
        out_specs=(half_a(rows), half_a(rows), half_a(rows)),
        compiler_params=pltpu.CompilerParams(
            dimension_semantics=("arbitrary",), vmem_limit_bytes=VMEM_LIMIT),
        name="up_conv_rows",
    )(h2d, prev2d, prev2d, w_up_bf16, w_up_bf16, conv_w, conv_w, cb2d, cb2d)


def _down_kernel(g_ref, x1_ref, wd_ref, gpost_ref, gt_ref, y_ref):
    f = _dot(g_ref[...], wd_ref[...])
    y_ref[...] = x1_ref[...] + gt_ref[...] * _rms(f, gpost_ref[...], NORM_EPS)


def _down_proj(g2d, x1, w_down_bf16, g_post, gt, tm, tiles_per_group):
    n = x1.shape[0]
    row_spec = pl.BlockSpec((tm, D_MODEL), lambda i: (i, 0))
    const2 = lambda i: (0, 0)
    return pl.pallas_call(
        _down_kernel,
        out_shape=jax.ShapeDtypeStruct((n, D_MODEL), F32),
        grid=(n // tm,),
        in_specs=[
            pl.BlockSpec((tm, D_FF), lambda i: (i, 0)),
            row_spec,
            pl.BlockSpec((D_FF, D_MODEL), const2),
            pl.BlockSpec((1, D_MODEL), const2),
            _mod_spec(gt.shape[1], tiles_per_group),
        ],
        out_specs=row_spec,
        compiler_params=pltpu.CompilerParams(
            dimension_semantics=("arbitrary",), vmem_limit_bytes=VMEM_LIMIT),
        name="down_proj",
    )(g2d, x1, w_down_bf16, g_post.reshape(1, D_MODEL), gt)


def _later_mask(n):
    idx = np.arange(n)
    return (idx[None, :] > idx[:, None]).astype(np.float32)


def _alibi_slopes_np():
    return (2.0 ** (-(8.0 / N_DIFF_HEADS) * np.arange(1, N_DIFF_HEADS + 1))).astype(np.float32)


def _alibi_features(slopes2):
    bf = ml_dtypes.bfloat16
    hi = slopes2.astype(bf).astype(np.float32)
    mid = (slopes2 - hi).astype(bf).astype(np.float32)
    lo = (slopes2 - hi - mid).astype(bf).astype(np.float32)
    sfeat = np.zeros((N_DIFF_HEADS, LANES, TQ), np.float32)
    for row, piece in enumerate((hi, mid, lo)):
        sfeat[:, row, :] = piece[:, None]
    kfeat = np.zeros((TQ, LANES), np.float32)
    kfeat[:, :3] = np.arange(TQ, dtype=np.float32)[:, None]
    return (jnp.asarray(sfeat.reshape(N_DIFF_HEADS // 2, 2, LANES, TQ), dtype=BF16),
            jnp.asarray(kfeat, dtype=BF16))


def kernel(x_prompt, x_sample, c_prompt, c_sample, cache_k, cache_v, state_conv, page_table,
           w_ada, b_ada, g_pre_attn, g_post_attn, w_in, w_out, lambda_q1, lambda_k1,
           lambda_q2, lambda_k2, g_subln, g_pre_mlp, g_post_mlp, w_up, conv_w, conv_b, w_down):
    layer = 0
    n_b, seq, _ = x_prompt.shape
    n_dec, dec_t, _ = x_sample.shape
    assert dec_t == DEC_T
    n_pool = cache_k.shape[1]

    w_in_b = w_in[layer].astype(BF16)
    w_out_b = w_out[layer].astype(BF16)
    w_up_b = w_up[layer].astype(BF16)
    w_down_b = w_down[layer].astype(BF16)

    c_all = jnp.concatenate([c_prompt, c_sample], axis=0)
    mod = _modulation(c_all, w_ada[layer], b_ada[layer])
    mod_p = [m.reshape(n_b, 1, D_MODEL) for m in jnp.split(mod[:n_b], N_MOD, axis=-1)]
    mod_s = [jnp.repeat(m, DEC_T, axis=0).reshape(1, n_dec * DEC_T, D_MODEL)
             for m in jnp.split(mod[n_b:], N_MOD, axis=-1)]

    slopes2 = _alibi_slopes_np() * np.float32(LOG2E)
    slope_pairs = jnp.asarray(np.repeat(slopes2, HEAD_DIM).reshape(N_DIFF_HEADS // 2, 1, LANES))
    slope_rep = jnp.asarray(np.broadcast_to(np.repeat(slopes2, 2 * DEC_T)[:, None],
                                            (DF_ROWS, PAGE_SIZE)))
    sfeat, kfeat = _alibi_features(slopes2)
    lam_vecs = jnp.stack([lambda_q1[layer], lambda_k1[layer],
                          lambda_q2[layer], lambda_k2[layer]]).astype(F32)
    g_sub = g_subln[layer].astype(F32)
    g_col = g_sub.reshape(HEAD_DIM, 1)
    g8 = jnp.tile(g_sub, N_DIFF_HEADS).reshape(1, DF_WIDTH)
    usuf = jnp.asarray(_later_mask(TQ), dtype=BF16)
    tri = _later_mask(PAGE_SIZE).T
    trio = jnp.asarray(np.concatenate([tri, np.ones_like(tri)], axis=1), dtype=BF16)

    tm_p = TQ
    tiles_p = seq // tm_p
    xp2d = x_prompt.reshape(n_b * seq, D_MODEL)
    sh_a, sc_a, gt_a, sh_m, sc_m, gt_m = mod_p
    qt_p, kt_p, vt_p, kb_p, vtb_p, kn2_p = _qkv_prompt(xp2d, sc_a, sh_a, g_pre_attn[layer], w_in_b,
                                                       n_b, seq, tm_p)
    kb4 = kb_p.reshape(n_b, tiles_p, TQ, D_MODEL)
    mixed_sb = _sb_prompt_attention(qt_p, kb4, vtb_p, usuf)
    mixed_df = _diff_prompt_attention(qt_p, kb4, vtb_p, slope_pairs, sfeat, kfeat,
                                      _prefix_key_norms(kn2_p, n_b, tiles_p), lam_vecs, g_col)
    tm_r = ROW_TILE
    tiles_r = seq // tm_r
    x1_p, h_p = _attn_out(mixed_sb.reshape(n_b * seq, SB_WIDTH), 0,
                          mixed_df.reshape(n_b * seq, DF_WIDTH), 0,
                          xp2d, w_out_b, g_post_attn[layer], gt_a, g_pre_mlp[layer], sc_m, sh_m,
                          tm_r, tiles_r)
    conv0 = jnp.zeros((n_b, CONV_WIDTH - 1, 2 * D_FF), F32)
    gate_p, conv_p = _up_conv(h_p.reshape(n_b, seq, D_MODEL), conv0, w_up_b,
                              conv_w[layer], conv_b[layer], tm_p)
    y_p = _down_proj(gate_p.reshape(n_b * seq, D_FF), x1_p, w_down_b, g_post_mlp[layer],
                     gt_m, tm_r, tiles_r)

    tm_s = n_dec * DEC_T
    xs2d = x_sample.reshape(tm_s, D_MODEL)
    sh_a, sc_a, gt_a, sh_m, sc_m, gt_m = mod_s
    q_s, k_s, v_s = _qkv_rows(xs2d, sc_a, sh_a, g_pre_attn[layer], w_in_b, tm_s)
    per_seq = lambda a: a.reshape(n_dec, DEC_T, D_MODEL)
    pages = lambda c: c[layer].transpose(0, 2, 3, 1).reshape(n_pool, D_MODEL, PAGE_SIZE)
    mixed_s = _dec_attention(
        page_table, per_seq(q_s), per_seq(k_s), per_seq(v_s), pages(cache_k), pages(cache_v),
        trio, slope_rep, lam_vecs, g8)
    mixed_s2d = mixed_s.reshape(tm_s, D_MODEL)
    x1_s, h_s = _attn_out(mixed_s2d, 0, mixed_s2d, 1, xs2d, w_out_b, g_post_attn[layer], gt_a,
                          g_pre_mlp[layer], sc_m, sh_m, tm_s, 1)
    prev_s = jnp.pad(state_conv[layer].astype(F32),
                     ((0, 0), (0, DEC_T - (CONV_WIDTH - 1)), (0, 0))).reshape(tm_s, 2 * D_FF)
    gate_s, ua_s, ub_s = _up_conv_rows(h_s, prev_s, w_up_b, conv_w[layer], conv_b[layer])
    conv_s = jnp.concatenate([ua_s, ub_s], axis=-1).reshape(n_dec, DEC_T, 2 * D_FF)[:, -2:]
    y_s = _down_proj(gate_s, x1_s, w_down_b, g_post_mlp[layer], gt_m, tm_s, 1)

    heads = lambda a, b, t: a.reshape(1, b, t, N_HEADS, HEAD_DIM)
    heads_t = lambda a: a.reshape(1, n_b, N_HEADS, HEAD_DIM, seq).transpose(0, 1, 4, 2, 3)
    return (y_p.reshape(n_b, seq, D_MODEL),
            y_s.reshape(n_dec, DEC_T, D_MODEL),
            heads_t(kt_p), heads_t(vt_p), conv_p[None],
            heads(k_s, n_dec, DEC_T), heads(v_s, n_dec, DEC_T), conv_s[None])
```

```python
import functools
import math

import jax
import jax.numpy as jnp
import ml_dtypes
import numpy as np
from jax import lax
from jax.experimental import pallas as pl
from jax.experimental.pallas import tpu as pltpu

F32 = jnp.float32
BF16 = jnp.bfloat16

D_MODEL = 1024
HEAD_DIM = 64
N_SB_HEADS = 8
N_DIFF_HEADS = 8
N_HEADS = N_SB_HEADS + N_DIFF_HEADS
SB_WIDTH = N_SB_HEADS * HEAD_DIM
DF_WIDTH = N_DIFF_HEADS * HEAD_DIM
DIFF_QK_DIM = HEAD_DIM // 2
D_FF = 2816
CONV_WIDTH = 3
PAGE_SIZE = 128
NORM_EPS = 1e-6
SUBLN_EPS = 1e-5
N_MOD = 6
LAMBDA_INIT = 0.8 - 0.6 * math.exp(-0.3 * 0)

LANES = 128
NEG_BIG = -1e30
LOG2E = math.log2(math.e)
DEAD_LOG2 = 104.0 * LOG2E + 1.0
BOUND_SLACK = 1.001

VMEM_LIMIT = 56 * 1024 * 1024
ROW_TILE = 512


def _dot(a, b):
    return jnp.dot(a, b, preferred_element_type=F32)


def _dot_nt(a, b):
    return lax.dot_general(a, b, (((1,), (1,)), ((), ())), preferred_element_type=F32)


def _rms(x, g, eps):
    return x * lax.rsqrt(jnp.mean(x * x, axis=-1, keepdims=True) + eps) * g


def _softplus2(z):
    return jnp.maximum(z, 0.0) + jnp.log2(1.0 + jnp.exp2(-jnp.abs(z)))


def _split_bf16(x):
    hi = x.astype(BF16)
    lo = (x - hi.astype(F32)).astype(BF16)
    return hi, lo


def _mod_kernel(c_ref, w_ref, b_ref, o_ref):
    c = c_ref[...]
    s = c * (1.0 / (1.0 + jnp.exp(-c)))
    o_ref[...] = _dot(s.astype(BF16), w_ref[...].astype(BF16)) + b_ref[...]


def _modulation(c_all, w_ada, b_ada):
    n_rows = c_all.shape[0]
    n_out = w_ada.shape[1]
    tn = 1536
    return pl.pallas_call(
        _mod_kernel,
        out_shape=jax.ShapeDtypeStruct((n_rows, n_out), F32),
        grid=(n_out // tn,),
        in_specs=[
            pl.BlockSpec((n_rows, D_MODEL), lambda j: (0, 0)),
            pl.BlockSpec((D_MODEL, tn), lambda j: (0, j)),
            pl.BlockSpec((1, tn), lambda j: (0, j)),
        ],
        out_specs=pl.BlockSpec((n_rows, tn), lambda j: (0, j)),
        compiler_params=pltpu.CompilerParams(
            dimension_semantics=("arbitrary",), vmem_limit_bytes=VMEM_LIMIT),
        name="adaln_mod",
    )(c_all, w_ada, b_ada.reshape(1, n_out))


def _scaled_qkv(x_ref, sc_ref, sh_ref, g_ref, w_ref):
    x = x_ref[...]
    h = _rms(x, g_ref[...], NORM_EPS) * (1.0 + sc_ref[...]) + sh_ref[...]
    qkv = _dot(h.astype(BF16), w_ref[...])
    lane = lax.broadcasted_iota(jnp.int32, (1, D_MODEL), 1)
    qscale = jnp.where(lane < SB_WIDTH, LOG2E * HEAD_DIM ** -0.5,
                       LOG2E * DIFF_QK_DIM ** -0.5).astype(F32)
    return qkv[:, :D_MODEL] * qscale, qkv[:, D_MODEL:2 * D_MODEL], qkv[:, 2 * D_MODEL:]


def _qkv_rows_kernel(x_ref, sc_ref, sh_ref, g_ref, w_ref, q_ref, k_ref, v_ref):
    q, k, v = _scaled_qkv(x_ref, sc_ref, sh_ref, g_ref, w_ref)
    q_ref[...] = q.astype(BF16)
    k_ref[...] = k
    v_ref[...] = v


def _qkv_prompt_kernel(x_ref, sc_ref, sh_ref, g_ref, w_ref, ind_ref,
                       qt_ref, kt_ref, vt_ref, kb_ref, vtb_ref, kn_ref):
    q, k, v = _scaled_qkv(x_ref, sc_ref, sh_ref, g_ref, w_ref)
    qt_ref[...] = q.T.astype(BF16)
    kt_ref[...] = k.T
    vt = v.T
    vt_ref[...] = vt
    vtb_ref[...] = vt.astype(BF16)
    kb = k.astype(BF16)
    kb_ref[...] = kb
    kf = kb.astype(F32)
    hi, lo = _split_bf16(kf * kf)
    ind = ind_ref[...]
    norms2 = _dot(hi, ind) + _dot(lo, ind)
    kn_ref[...] = jnp.max(norms2, axis=0, keepdims=True) * (1.0 + 2.0 ** -12)


def _mod_spec(mod_rows, tiles_per_group):
    return pl.BlockSpec((None, mod_rows, D_MODEL), lambda i: (i // tiles_per_group, 0, 0))


def _qkv_in_specs(sc, sh, tm, tiles_per_group):
    const2 = lambda i: (0, 0)
    return [
        pl.BlockSpec((tm, D_MODEL), lambda i: (i, 0)),
        _mod_spec(sc.shape[1], tiles_per_group),
        _mod_spec(sh.shape[1], tiles_per_group),
        pl.BlockSpec((1, D_MODEL), const2),
        pl.BlockSpec((D_MODEL, 3 * D_MODEL), const2),
    ]


def _qkv_rows(x2d, sc, sh, g, w_bf16, tm):
    n = x2d.shape[0]
    row_spec = pl.BlockSpec((tm, D_MODEL), lambda i: (i, 0))
    out_f32 = jax.ShapeDtypeStruct((n, D_MODEL), F32)
    return pl.pallas_call(
        _qkv_rows_kernel,
        out_shape=(jax.ShapeDtypeStruct((n, D_MODEL), BF16), out_f32, out_f32),
        grid=(n // tm,),
        in_specs=_qkv_in_specs(sc, sh, tm, 1),
        out_specs=(row_spec, row_spec, row_spec),
        compiler_params=pltpu.CompilerParams(
            dimension_semantics=("arbitrary",), vmem_limit_bytes=VMEM_LIMIT),
        name="qkv_rows",
    )(x2d, sc, sh, g.reshape(1, D_MODEL), w_bf16)


def _qkv_prompt(x2d, sc, sh, g, w_bf16, n_b, seq, tm):
    tiles = seq // tm
    t_spec = pl.BlockSpec((None, D_MODEL, tm), lambda i: (i // tiles, 0, i % tiles))
    t_f32 = jax.ShapeDtypeStruct((n_b, D_MODEL, seq), F32)
    group_of_lane = np.arange(D_MODEL)[:, None] // DIFF_QK_DIM == np.arange(LANES)[None, :]
    ind = jnp.asarray(group_of_lane.astype(np.float32), dtype=BF16)
    return pl.pallas_call(
        _qkv_prompt_kernel,
        out_shape=(jax.ShapeDtypeStruct((n_b, D_MODEL, seq), BF16), t_f32, t_f32,
                   jax.ShapeDtypeStruct((n_b * seq, D_MODEL), BF16),
                   jax.ShapeDtypeStruct((n_b, tiles, D_MODEL, tm), BF16),
                   jax.ShapeDtypeStruct((n_b * tiles, 1, LANES), F32)),
        grid=(n_b * tiles,),
        in_specs=_qkv_in_specs(sc, sh, tm, tiles) + [pl.BlockSpec((D_MODEL, LANES), lambda i: (0, 0))],
        out_specs=(t_spec, t_spec, t_spec,
                   pl.BlockSpec((tm, D_MODEL), lambda i: (i, 0)),
                   pl.BlockSpec((None, None, D_MODEL, tm), lambda i: (i // tiles, i % tiles, 0, 0)),
                   pl.BlockSpec((None, 1, LANES), lambda i: (i, 0, 0))),
        compiler_params=pltpu.CompilerParams(
            dimension_semantics=("arbitrary",), vmem_limit_bytes=VMEM_LIMIT),
        name="qkv_prompt",
    )(x2d, sc, sh, g.reshape(1, D_MODEL), w_bf16, ind)


def _prefix_key_norms(kn2, n_b, tiles):
    n_groups = 2 * N_DIFF_HEADS
    first = SB_WIDTH // DIFF_QK_DIM
    kn = jnp.sqrt(kn2.reshape(n_b, tiles, LANES)[:, :, first:first + n_groups])
    kn = lax.cummax(kn, axis=1)
    kn = kn.reshape(n_b, tiles, N_DIFF_HEADS // 2, 4).transpose(0, 2, 1, 3)
    return jnp.pad(kn, ((0, 0), (0, 0), (0, 0), (0, LANES - 4)))


def _lambda_value(lam_ref):
    lq1 = lam_ref[0:1, :]
    lk1 = lam_ref[1:2, :]
    lq2 = lam_ref[2:3, :]
    lk2 = lam_ref[3:4, :]
    return (jnp.exp(jnp.sum(lq1 * lk1, axis=1, keepdims=True))
            - jnp.exp(jnp.sum(lq2 * lk2, axis=1, keepdims=True)) + LAMBDA_INIT)


TQ = 256
SB_STEP_HEADS = 4
SB_STEP_WIDTH = SB_STEP_HEADS * HEAD_DIM
ONES_ROWS = 16
DIFF_TRIPS = ((1, 1), (2, 1), (4, 1), (16, 1 << 30), (8, 1), (4, 1), (2, 1), (1, 1))
DIFF_LEAD_TIERS = 3
DIFF_STEEP_PAIRS = 2


def _prompt_specs(s, first_group, width):
    n_blk = s // TQ
    return [
        pl.BlockSpec((None, width, TQ), lambda bi, p, i: (bi, first_group + p, i)),
        pl.BlockSpec((None, n_blk, TQ, width), lambda bi, p, i: (bi, 0, 0, first_group + p)),
        pl.BlockSpec((None, n_blk, width, TQ), lambda bi, p, i: (bi, 0, first_group + p, 0)),
    ]


def _sb_prompt_kernel(qt_ref, k_ref, vt_ref, usuf_ref, o_ref, acc_ref):
    i = pl.program_id(2)
    heads = range(SB_STEP_HEADS)
    qt = qt_ref[...].astype(F32)
    drow = lax.broadcasted_iota(jnp.int32, (SB_STEP_WIDTH, TQ), 0)
    key = lax.broadcasted_iota(jnp.int32, (TQ, TQ), 0)
    qry = lax.broadcasted_iota(jnp.int32, (TQ, TQ), 1)
    strictly_causal = key < qry
    usuf = usuf_ref[...]
    qtm = [jnp.where((drow >= HEAD_DIM * h) & (drow < HEAD_DIM * (h + 1)), qt, 0.0).astype(BF16)
           for h in heads]

    def step(js, runs, masked_first):
        streams = [(b, h) for b in range(len(js)) for h in heads]
        kbs = [k_ref[j] for j in js]
        vts = [vt_ref[j] for j in js]
        zs = {(b, h): _dot(kbs[b], qtm[h]) for b, h in streams}
        log_sig, parts, offsets = {}, {}, {}
        runs = list(runs)
        for b, h in streams:
            masked = masked_first and b == 0
            sp = _softplus2(zs[b, h])
            log_sig[b, h] = zs[b, h] - sp
            if masked:
                sp = jnp.where(strictly_causal, sp, 0.0)
            parts[b, h] = _split_bf16(sp)
            offsets[b, h] = runs[h]
            runs[h] = runs[h] + jnp.sum(sp, axis=0, keepdims=True)
        between = {s: _dot(usuf, parts[s][0]) + _dot(usuf, parts[s][1]) for s in streams}
        weights = {}
        for b, h in streams:
            w = jnp.exp2(log_sig[b, h] - between[b, h] - offsets[b, h])
            if masked_first and b == 0:
                w = jnp.where(strictly_causal, w, 0.0)
            weights[b, h] = w.astype(BF16)
        outs = {(b, h): _dot(vts[b][HEAD_DIM * h:HEAD_DIM * (h + 1)], weights[b, h])
                for b, h in streams}
        for h in heads:
            acc = acc_ref[h]
            for b in range(len(js)):
                acc = acc + outs[b, h]
            acc_ref[h] = acc
        return tuple(runs)

    def alive(runs):
        least = runs[0]
        for r in runs[1:]:
            least = jnp.minimum(least, r)
        return (jnp.min(least) < DEAD_LOG2).astype(jnp.int32)

    acc_ref[...] = jnp.zeros_like(acc_ref)
    zeros = (jnp.zeros((1, TQ), F32),) * SB_STEP_HEADS
    runs = lax.cond(i > 0,
                    lambda: step([i, i - 1], zeros, True),
                    lambda: step([i], zeros, True))

    def cond(c):
        return (c[0] < i) & (c[1] > 0)

    def body(c):
        runs = step([i - 1 - c[0]], c[2:], False)
        return (c[0] + 1, alive(runs)) + runs

    lax.while_loop(cond, body, (jnp.int32(1), alive(runs)) + runs)
    out_t = jnp.concatenate([acc_ref[h] for h in heads], axis=0)
    o_ref[...] = out_t.T.astype(BF16)


def _sb_prompt_attention(qt, kb4, vtb4, usuf):
    b, _, s = qt.shape
    return pl.pallas_call(
        _sb_prompt_kernel,
        out_shape=jax.ShapeDtypeStruct((b, s, SB_WIDTH), BF16),
        grid=(b, SB_WIDTH // SB_STEP_WIDTH, s // TQ),
        in_specs=_prompt_specs(s, 0, SB_STEP_WIDTH)
        + [pl.BlockSpec((TQ, TQ), lambda bi, p, i: (0, 0))],
        out_specs=pl.BlockSpec((None, TQ, SB_STEP_WIDTH), lambda bi, p, i: (bi, i, p)),
        scratch_shapes=[pltpu.VMEM((SB_STEP_HEADS, HEAD_DIM, TQ), F32)],
        compiler_params=pltpu.CompilerParams(
            dimension_semantics=("arbitrary", "arbitrary", "arbitrary"),
            vmem_limit_bytes=VMEM_LIMIT),
        name="sb_prompt_attn",
    )(qt, kb4, vtb4, usuf)


def _diff_prompt_kernel(qt_ref, k_ref, vt_ref, slope_ref, sfeat_ref, kfeat_ref, kmax_ref, lam_ref,
                        g_ref, o_ref, acc_ref):
    i = pl.program_id(2)
    qt = qt_ref[...].astype(F32)
    drow = lax.broadcasted_iota(jnp.int32, (LANES, TQ), 0)
    key = lax.broadcasted_iota(jnp.int32, (TQ, TQ), 0)
    qry = lax.broadcasted_iota(jnp.int32, (TQ, TQ), 1)
    causal = key <= qry
    lam = _lambda_value(lam_ref)
    slopes = [slope_ref[:, HEAD_DIM * h:HEAD_DIM * h + 1] for h in range(2)]
    kfeat = kfeat_ref[...]
    qext = []
    for h in range(2):
        for c in range(2):
            lo_row = HEAD_DIM * h + DIFF_QK_DIM * c
            in_map = (drow >= lo_row) & (drow < lo_row + DIFF_QK_DIM)
            qtm = jnp.where(in_map, qt, 0.0).astype(BF16)
            qext.append(jnp.concatenate([qtm, sfeat_ref[h]], axis=0))

    ones_rows = jnp.ones((ONES_ROWS, TQ), BF16)

    def step(js, ms, masked):
        ms = list(ms)
        kexts = [jnp.concatenate([k_ref[j], kfeat], axis=1) for j in js]
        scores = [[_dot(kext, qext[idx]) for idx in range(4)] for kext in kexts]
        pending = []
        for b, j in enumerate(js):
            alphas, probs = [], []
            for idx in range(4):
                h = idx // 2
                s = scores[b][idx]
                if masked:
                    s = jnp.where(causal, s, NEG_BIG)
                shift = slopes[h] * ((i - j) * TQ).astype(F32)
                m_new = jnp.maximum(ms[idx], jnp.max(s, axis=0, keepdims=True) - shift)
                alphas.append(jnp.exp2(ms[idx] - m_new))
                probs.append(jnp.exp2(s - (m_new + shift)).astype(BF16))
                ms[idx] = m_new
            vt = vt_ref[j]
            vts = [jnp.concatenate([vt[HEAD_DIM * h:HEAD_DIM * (h + 1)], ones_rows], axis=0)
                   for h in range(2)]
            pending.append((alphas, [_dot(vts[idx // 2], probs[idx]) for idx in range(4)]))
        for idx in range(4):
            acc = acc_ref[idx]
            for alphas, outs in pending:
                acc = alphas[idx] * acc + outs[idx]
            acc_ref[idx] = acc
        return tuple(ms)

    q_norm = []
    for idx in range(4):
        lo_row = DIFF_QK_DIM * idx
        in_map = (drow >= lo_row) & (drow < lo_row + DIFF_QK_DIM)
        q_norm.append(jnp.sqrt(jnp.sum(jnp.where(in_map, qt * qt, 0.0), axis=0, keepdims=True)))

    def alive(rem, ms):
        k_norm = kmax_ref[pl.ds(jnp.maximum(rem - 1, 0), 1), :]
        gap = ((TQ - 1) - (i - rem + 1) * TQ).astype(F32)
        worst = jnp.full((1, TQ), NEG_BIG, F32)
        for idx in range(4):
            bound = q_norm[idx] * k_norm[:, idx:idx + 1] * BOUND_SLACK + slopes[idx // 2] * gap
            worst = jnp.maximum(worst, bound - ms[idx])
        return (jnp.max(worst) > -DEAD_LOG2).astype(jnp.int32)

    acc_ref[...] = jnp.zeros_like(acc_ref)
    ms = step([i], (jnp.full((1, TQ), NEG_BIG, F32),) * 4, True)

    state = (i, alive(i, ms)) + ms
    steep = (pl.program_id(1) < DIFF_STEEP_PAIRS).astype(jnp.int32)
    for tier, (group, max_trips) in enumerate(DIFF_TRIPS):
        if tier < DIFF_LEAD_TIERS:
            max_trips = max_trips * steep

        def cond(c, group=group, max_trips=max_trips):
            return (c[0] < max_trips) & (c[1] >= group) & (c[2] > 0)

        def body(c, group=group):
            rem, ms = c[1], c[3:]
            still = alive(rem - group, ms)
            ms = step([rem - group + u for u in range(group)], ms, False)
            return (c[0] + 1, rem - group, still) + ms

        state = lax.while_loop(cond, body, (jnp.int32(0),) + state)[1:]

    y_t = []
    for h in range(2):
        maps = [acc_ref[2 * h + c, :HEAD_DIM, :] / acc_ref[2 * h + c, HEAD_DIM:HEAD_DIM + 1, :]
                for c in range(2)]
        o_t = maps[0] - lam * maps[1]
        ms_t = jnp.mean(o_t * o_t, axis=0, keepdims=True)
        y_t.append(o_t * lax.rsqrt(ms_t + SUBLN_EPS) * g_ref[...] * (1.0 - LAMBDA_INIT))
    o_ref[...] = jnp.concatenate(y_t, axis=0).T.astype(BF16)


def _diff_prompt_attention(qt, kb4, vtb4, slopes, sfeat, kfeat, kmax, lam_vecs, g_col):
    b, _, s = qt.shape
    n_pairs = DF_WIDTH // LANES
    const2 = lambda bi, p, i: (0, 0)
    return pl.pallas_call(
        _diff_prompt_kernel,
        out_shape=jax.ShapeDtypeStruct((b, s, DF_WIDTH), BF16),
        grid=(b, n_pairs, s // TQ),
        in_specs=_prompt_specs(s, SB_WIDTH // LANES, LANES) + [
            pl.BlockSpec((None, 1, LANES), lambda bi, p, i: (p, 0, 0)),
            pl.BlockSpec((None, 2, LANES, TQ), lambda bi, p, i: (p, 0, 0, 0)),
            pl.BlockSpec((TQ, LANES), const2),
            pl.BlockSpec((None, None, s // TQ, LANES), lambda bi, p, i: (bi, p, 0, 0)),
            pl.BlockSpec((4, DIFF_QK_DIM), const2),
            pl.BlockSpec((HEAD_DIM, 1), const2),
        ],
        out_specs=pl.BlockSpec((None, TQ, LANES), lambda bi, p, i: (bi, i, p)),
        scratch_shapes=[pltpu.VMEM((4, HEAD_DIM + ONES_ROWS, TQ), F32)],
        compiler_params=pltpu.CompilerParams(
            dimension_semantics=("arbitrary", "arbitrary", "arbitrary"),
            vmem_limit_bytes=VMEM_LIMIT),
        name="diff_prompt_attn",
    )(qt, kb4, vtb4, slopes, sfeat, kfeat, kmax, lam_vecs, g_col)


PAGES_PER_STEP = 16
DEC_T = 8
SB_ROWS = N_SB_HEADS * DEC_T
DF_ROWS = N_DIFF_HEADS * 2 * DEC_T


def _group_rms_lanes(o, g, lane, n_groups):
    sq = o * o
    ms = jnp.zeros_like(o)
    for h in range(n_groups):
        in_h = (lane >= HEAD_DIM * h) & (lane < HEAD_DIM * (h + 1))
        s_h = jnp.sum(jnp.where(in_h, sq, 0.0), axis=1, keepdims=True)
        ms = jnp.where(in_h, s_h * (1.0 / HEAD_DIM), ms)
    return o * lax.rsqrt(ms + SUBLN_EPS) * g


def _dec_attn_kernel(pt_ref, q_ref, kn_ref, vn_ref, *rest, past_len):
    del pt_ref
    kp_refs = rest[:PAGES_PER_STEP]
    vp_refs = rest[PAGES_PER_STEP:2 * PAGES_PER_STEP]
    (trio_ref, slope_ref, lam_ref, g_ref, o_ref, wsb_ref, wdf_ref,
     run_ref, accsb_ref, mcol_ref, mrep_ref, l_ref, accdf_ref) = rest[2 * PAGES_PER_STEP:]
    step_id = pl.program_id(1)
    n_steps = pl.num_programs(1)

    key_lane = lax.broadcasted_iota(jnp.int32, (DF_ROWS, PAGE_SIZE), 1)
    t_df = lax.broadcasted_iota(jnp.int32, (DF_ROWS, PAGE_SIZE), 0) & (DEC_T - 1)
    slope = slope_ref[...]
    bias0 = slope * (key_lane - past_len - t_df).astype(F32)
    trio = trio_ref[...]

    def sb_group(chunks):
        offset = run_ref[...]
        acc = accsb_ref[...]
        parts = []
        for z, _, mask in chunks:
            sp = _softplus2(z)
            log_sig = z - sp
            if mask is not None:
                sp = jnp.where(mask, sp, 0.0)
            parts.append((log_sig, _split_bf16(sp)))
        sums = [_dot(hi, trio) + _dot(lo, trio) for _, (hi, lo) in parts]
        for (_, pv, mask), (log_sig, _), sm in zip(chunks, parts, sums):
            w = jnp.exp2(log_sig - sm[:, :PAGE_SIZE] - offset)
            if mask is not None:
                w = jnp.where(mask, w, 0.0)
            acc = acc + pv(w.astype(BF16))
            offset = offset + sm[:, PAGE_SIZE:]
        run_ref[...] = offset
        accsb_ref[...] = acc

    def df_group(chunks):
        top = chunks[0][0]
        for s, _ in chunks[1:]:
            top = jnp.maximum(top, s)
        m_old = mcol_ref[...]
        m_new = jnp.maximum(m_old, jnp.max(top, axis=1, keepdims=True))
        m_rep_old = mrep_ref[...]
        m_rep = jnp.broadcast_to(m_new, (DF_ROWS, PAGE_SIZE))
        p_sum = jnp.zeros((DF_ROWS, PAGE_SIZE), F32)
        pv_sum = jnp.zeros((DF_ROWS, DF_WIDTH), F32)
        for s, pv in chunks:
            p = jnp.exp2(s - m_rep)
            p_sum = p_sum + p
            pv_sum = pv_sum + pv(p.astype(BF16))
        alpha_rep = jnp.exp2(m_rep_old - m_rep)
        l_ref[...] = jnp.exp2(m_old - m_new) * l_ref[...] + jnp.sum(p_sum, axis=1, keepdims=True)
        accdf_ref[...] = jnp.concatenate([alpha_rep] * (DF_WIDTH // PAGE_SIZE), axis=1) * accdf_ref[...] + pv_sum
        mcol_ref[...] = m_new
        mrep_ref[...] = m_rep

    @pl.when(step_id == 0)
    def _init():
        qf = q_ref[...].astype(F32)
        q_sb = jnp.concatenate([qf[:, :SB_WIDTH]] * N_SB_HEADS, axis=0)
        r = lax.broadcasted_iota(jnp.int32, (SB_ROWS, SB_WIDTH), 0)
        c = lax.broadcasted_iota(jnp.int32, (SB_ROWS, SB_WIDTH), 1)
        wsb_ref[...] = jnp.where((r >> 3) == (c >> 6), q_sb, 0.0).astype(BF16)
        q_df = jnp.concatenate([qf[:, SB_WIDTH:]] * (2 * N_DIFF_HEADS), axis=0)
        r = lax.broadcasted_iota(jnp.int32, (DF_ROWS, DF_WIDTH), 0)
        c = lax.broadcasted_iota(jnp.int32, (DF_ROWS, DF_WIDTH), 1)
        wdf_ref[...] = jnp.where((r >> 3) == (c >> 5), q_df, 0.0).astype(BF16)
        run_ref[...] = jnp.zeros_like(run_ref)
        accsb_ref[...] = jnp.zeros_like(accsb_ref)
        mcol_ref[...] = jnp.full_like(mcol_ref, NEG_BIG)
        mrep_ref[...] = jnp.full_like(mrep_ref, NEG_BIG)
        l_ref[...] = jnp.zeros_like(l_ref)
        accdf_ref[...] = jnp.zeros_like(accdf_ref)
        pad = jnp.zeros((PAGE_SIZE - DEC_T, D_MODEL), F32)
        kn = jnp.concatenate([kn_ref[...], pad], axis=0).astype(BF16)
        vn = jnp.concatenate([vn_ref[...], pad], axis=0).astype(BF16)
        key_sb = lax.broadcasted_iota(jnp.int32, (SB_ROWS, PAGE_SIZE), 1)
        t_sb = lax.broadcasted_iota(jnp.int32, (SB_ROWS, PAGE_SIZE), 0) & (DEC_T - 1)
        sb_group([(_dot_nt(wsb_ref[...], kn[:, :SB_WIDTH]),
                   lambda w: _dot(w, vn[:, :SB_WIDTH]), key_sb < t_sb)])
        s_new = _dot_nt(wdf_ref[...], kn[:, SB_WIDTH:]) + (bias0 + slope * float(past_len))
        df_group([(jnp.where(key_lane <= t_df, s_new, NEG_BIG), lambda p: _dot(p, vn[:, SB_WIDTH:]))])

    n_pages = n_steps * PAGES_PER_STEP
    wsb = wsb_ref[...]
    wdf = wdf_ref[...]
    sb_chunks, df_chunks = [], []
    for r in range(PAGES_PER_STEP):
        page = n_pages - 1 - (step_id * PAGES_PER_STEP + r)
        kp = kp_refs[r][...].astype(BF16)
        vp = vp_refs[r][...].astype(BF16)
        sb_chunks.append((_dot(wsb, kp[:SB_WIDTH, :]),
                          functools.partial(_dot_nt, b=vp[:SB_WIDTH, :]), None))
        base = (page * PAGE_SIZE).astype(F32)
        df_chunks.append((_dot(wdf, kp[SB_WIDTH:, :]) + (bias0 + slope * base),
                          functools.partial(_dot_nt, b=vp[SB_WIDTH:, :])))
    sb_group(sb_chunks)
    df_group(df_chunks)

    @pl.when(step_id == n_steps - 1)
    def _finish():
        lane = lax.broadcasted_iota(jnp.int32, (DEC_T, SB_WIDTH), 1)
        lam = _lambda_value(lam_ref)
        acc_sb = accsb_ref[...]
        acc_df = accdf_ref[...] / l_ref[...]
        o_sb = jnp.zeros((DEC_T, SB_WIDTH), F32)
        o_df = jnp.zeros((DEC_T, DF_WIDTH), F32)
        for h in range(N_SB_HEADS):
            in_h = (lane >> 6) == h
            o_sb = jnp.where(in_h, acc_sb[DEC_T * h:DEC_T * (h + 1), :], o_sb)
            r0 = 2 * DEC_T * h
            o_h = acc_df[r0:r0 + DEC_T, :] - lam * acc_df[r0 + DEC_T:r0 + 2 * DEC_T, :]
            o_df = jnp.where(in_h, o_h, o_df)
        y_df = _group_rms_lanes(o_df, g_ref[...], lane, N_DIFF_HEADS) * (1.0 - LAMBDA_INIT)
        o_ref[:, :SB_WIDTH] = o_sb.astype(BF16)
        o_ref[:, SB_WIDTH:] = y_df.astype(BF16)


def _dec_attention(page_table, q, k_new, v_new, cache_k, cache_v, trio, slope_rep, lam_vecs, g8):
    n_seq, n_pages = page_table.shape
    past_len = n_pages * PAGE_SIZE
    n_steps = n_pages // PAGES_PER_STEP

    def page_spec(r):
        def idx(b, s, pt):
            return (pt[b, n_pages - 1 - (s * PAGES_PER_STEP + r)], 0, 0)
        return pl.BlockSpec((None, D_MODEL, PAGE_SIZE), idx)

    seq_spec = pl.BlockSpec((None, DEC_T, D_MODEL), lambda b, s, pt: (b, 0, 0))
    const2 = lambda b, s, pt: (0, 0)
    grid_spec = pltpu.PrefetchScalarGridSpec(
        num_scalar_prefetch=1,
        grid=(n_seq, n_steps),
        in_specs=[seq_spec, seq_spec, seq_spec]
        + [page_spec(r) for r in range(PAGES_PER_STEP)]
        + [page_spec(r) for r in range(PAGES_PER_STEP)]
        + [
            pl.BlockSpec((PAGE_SIZE, 2 * PAGE_SIZE), const2),
            pl.BlockSpec((DF_ROWS, PAGE_SIZE), const2),
            pl.BlockSpec((4, DIFF_QK_DIM), const2),
            pl.BlockSpec((1, DF_WIDTH), const2),
        ],
        out_specs=seq_spec,
        scratch_shapes=[
            pltpu.VMEM((SB_ROWS, SB_WIDTH), BF16),
            pltpu.VMEM((DF_ROWS, DF_WIDTH), BF16),
            pltpu.VMEM((SB_ROWS, PAGE_SIZE), F32),
            pltpu.VMEM((SB_ROWS, SB_WIDTH), F32),
            pltpu.VMEM((DF_ROWS, 1), F32),
            pltpu.VMEM((DF_ROWS, PAGE_SIZE), F32),
            pltpu.VMEM((DF_ROWS, 1), F32),
            pltpu.VMEM((DF_ROWS, DF_WIDTH), F32),
        ],
    )
    return pl.pallas_call(
        functools.partial(_dec_attn_kernel, past_len=past_len),
        out_shape=jax.ShapeDtypeStruct((n_seq, DEC_T, D_MODEL), BF16),
        grid_spec=grid_spec,
        compiler_params=pltpu.CompilerParams(
            dimension_semantics=("arbitrary", "arbitrary"), vmem_limit_bytes=VMEM_LIMIT),
        name="dec_attn",
    )(page_table, q, k_new, v_new, *([cache_k] * PAGES_PER_STEP), *([cache_v] * PAGES_PER_STEP),
      trio, slope_rep, lam_vecs, g8)


def _attn_out_kernel(ma_ref, mb_ref, x_ref, wa_ref, wb_ref, gpost_ref, gt_ref,
                     gpre_ref, sc_ref, sh_ref, x1_ref, h_ref):
    y = _dot(ma_ref[...], wa_ref[...]) + _dot(mb_ref[...], wb_ref[...])
    x1 = x_ref[...] + gt_ref[...] * _rms(y, gpost_ref[...], NORM_EPS)
    x1_ref[...] = x1
    h = _rms(x1, gpre_ref[...], NORM_EPS) * (1.0 + sc_ref[...]) + sh_ref[...]
    h_ref[...] = h.astype(BF16)


def _attn_out(mixed_a, a_col, mixed_b, b_col, x2d, w_out_bf16, g_post, gt, g_pre, sc, sh,
              tm, tiles_per_group):
    n = x2d.shape[0]
    half = D_MODEL // 2
    row_spec = pl.BlockSpec((tm, D_MODEL), lambda i: (i, 0))
    const2 = lambda i: (0, 0)
    vec_spec = pl.BlockSpec((1, D_MODEL), const2)
    mod = lambda a: _mod_spec(a.shape[1], tiles_per_group)
    return pl.pallas_call(
        _attn_out_kernel,
        out_shape=(jax.ShapeDtypeStruct((n, D_MODEL), F32),
                   jax.ShapeDtypeStruct((n, D_MODEL), BF16)),
        grid=(n // tm,),
        in_specs=[
            pl.BlockSpec((tm, half), lambda i: (i, a_col)),
            pl.BlockSpec((tm, half), lambda i: (i, b_col)),
            row_spec,
            pl.BlockSpec((half, D_MODEL), lambda i: (0, 0)),
            pl.BlockSpec((half, D_MODEL), lambda i: (1, 0)),
            vec_spec, mod(gt), vec_spec, mod(sc), mod(sh),
        ],
        out_specs=(row_spec, row_spec),
        compiler_params=pltpu.CompilerParams(
            dimension_semantics=("arbitrary",), vmem_limit_bytes=VMEM_LIMIT),
        name="attn_out",
    )(mixed_a, mixed_b, x2d, w_out_bf16, w_out_bf16, g_post.reshape(1, D_MODEL), gt,
      g_pre.reshape(1, D_MODEL), sc, sh)


UP_CHUNK = 1408


def _up_conv_kernel(h_ref, prev_ref, wup_ref, cw_ref, cb_ref, g_ref, st_ref, carry_ref, *, tm):
    t = pl.program_id(1)

    @pl.when(t == 0)
    def _load_state():
        carry_ref[...] = prev_ref[...]

    h = h_ref[...]
    row = lax.broadcasted_iota(jnp.int32, (tm, 1), 0)

    def conv_cols(c0):
        u = _dot(h, wup_ref[:, c0:c0 + UP_CHUNK])
        p0 = carry_ref[0:1, c0:c0 + UP_CHUNK]
        p1 = carry_ref[1:2, c0:c0 + UP_CHUNK]
        u1 = jnp.where(row == 0, p1, pltpu.roll(u, 1, 0))
        u2 = jnp.where(row == 0, p0, jnp.where(row == 1, p1, pltpu.roll(u, 2, 0)))
        w0 = cw_ref[0:1, c0:c0 + UP_CHUNK]
        w1 = cw_ref[1:2, c0:c0 + UP_CHUNK]
        w2 = cw_ref[2:3, c0:c0 + UP_CHUNK]
        conv = cb_ref[:, c0:c0 + UP_CHUNK] + w0 * u2 + w1 * u1 + w2 * u
        last = u[tm - 2:tm, :]
        carry_ref[:, c0:c0 + UP_CHUNK] = last
        st_ref[:, c0:c0 + UP_CHUNK] = last
        return conv

    for ch in range(D_FF // UP_CHUNK):
        a = conv_cols(ch * UP_CHUNK)
        b = conv_cols(D_FF + ch * UP_CHUNK)
        gate = a * (1.0 / (1.0 + jnp.exp(-a))) * b
        g_ref[:, ch * UP_CHUNK:(ch + 1) * UP_CHUNK] = gate.astype(BF16)


def _up_conv(h3d, conv_prev, w_up_bf16, conv_w, conv_b, tm):
    nb, t_len, _ = h3d.shape
    const2 = lambda b, t: (0, 0)
    return pl.pallas_call(
        functools.partial(_up_conv_kernel, tm=tm),
        out_shape=(jax.ShapeDtypeStruct((nb, t_len, D_FF), BF16),
                   jax.ShapeDtypeStruct((nb, CONV_WIDTH - 1, 2 * D_FF), F32)),
        grid=(nb, t_len // tm),
        in_specs=[
            pl.BlockSpec((None, tm, D_MODEL), lambda b, t: (b, t, 0)),
            pl.BlockSpec((None, CONV_WIDTH - 1, 2 * D_FF), lambda b, t: (b, 0, 0)),
            pl.BlockSpec((D_MODEL, 2 * D_FF), const2),
            pl.BlockSpec((CONV_WIDTH, 2 * D_FF), const2),
            pl.BlockSpec((1, 2 * D_FF), const2),
        ],
        out_specs=(pl.BlockSpec((None, tm, D_FF), lambda b, t: (b, t, 0)),
                   pl.BlockSpec((None, CONV_WIDTH - 1, 2 * D_FF), lambda b, t: (b, 0, 0))),
        scratch_shapes=[pltpu.VMEM((CONV_WIDTH - 1, 2 * D_FF), F32)],
        compiler_params=pltpu.CompilerParams(
            dimension_semantics=("arbitrary", "arbitrary"), vmem_limit_bytes=VMEM_LIMIT),
        name="up_conv_gate",
    )(h3d, conv_prev, w_up_bf16, conv_w, conv_b.reshape(1, 2 * D_FF))


def _up_conv_rows_kernel(h_ref, pa_ref, pb_ref, wa_ref, wb_ref, cwa_ref, cwb_ref, cba_ref, cbb_ref,
                         g_ref, ua_ref, ub_ref):
    h = h_ref[...]
    rows = h.shape[0]
    t = lax.broadcasted_iota(jnp.int32, (rows, 1), 0) & (DEC_T - 1)

    def conv(w_ref, prev_ref, cw_ref, cb_ref, u_ref):
        u = _dot(h, w_ref[...])
        u_ref[...] = u
        prev = prev_ref[...]
        u1 = jnp.where(t == 0, pltpu.roll(prev, rows - 1, 0), pltpu.roll(u, 1, 0))
        u2 = jnp.where(t < 2, prev, pltpu.roll(u, 2, 0))
        return cb_ref[...] + cw_ref[0:1, :] * u2 + cw_ref[1:2, :] * u1 + cw_ref[2:3, :] * u

    a = conv(wa_ref, pa_ref, cwa_ref, cba_ref, ua_ref)
    b = conv(wb_ref, pb_ref, cwb_ref, cbb_ref, ub_ref)
    g_ref[...] = (a * (1.0 / (1.0 + jnp.exp(-a))) * b).astype(BF16)


def _up_conv_rows(h2d, prev2d, w_up_bf16, conv_w, conv_b):
    rows = h2d.shape[0]
    n_chunks = D_FF // UP_CHUNK
    half_a = lambda r: pl.BlockSpec((r, UP_CHUNK), lambda c: (0, c))
    half_b = lambda r: pl.BlockSpec((r, UP_CHUNK), lambda c: (0, n_chunks + c))
    u_half = jax.ShapeDtypeStruct((rows, D_FF), F32)
    cb2d = conv_b.reshape(1, 2 * D_FF)
    return pl.pallas_call(
        _up_conv_rows_kernel,
        out_shape=(jax.ShapeDtypeStruct((rows, D_FF), BF16), u_half, u_half),
        grid=(n_chunks,),
        in_specs=[pl.BlockSpec((rows, D_MODEL), lambda c: (0, 0)),
                  half_a(rows), half_b(rows), half_a(D_MODEL), half_b(D_MODEL),
                  half_a(CONV_WIDTH), half_b(CONV_WIDTH), half_a(1), half_b(1)],
        out_specs=(half_a(rows), half_a(rows), half_a(rows)),
        compiler_params=pltpu.CompilerParams(
            dimension_semantics=("arbitrary",), vmem_limit_bytes=VMEM_LIMIT),
        name="up_conv_rows",
    )(h2d, prev2d, prev2d, w_up_bf16, w_up_bf16, conv_w, conv_w, cb2d, cb2d)


def _down_kernel(g_ref, x1_ref, wd_ref, gpost_ref, gt_ref, y_ref):
    f = _dot(g_ref[...], wd_ref[...])
    y_ref[...] = x1_ref[...] + gt_ref[...] * _rms(f, gpost_ref[...], NORM_EPS)


def _down_proj(g2d, x1, w_down_bf16, g_post, gt, tm, tiles_per_group):
    n = x1.shape[0]
    row_spec = pl.BlockSpec((tm, D_MODEL), lambda i: (i, 0))
    const2 = lambda i: (0, 0)
    return pl.pallas_call(
        _down_kernel,
        out_shape=jax.ShapeDtypeStruct((n, D_MODEL), F32),
        grid=(n // tm,),
        in_specs=[
            pl.BlockSpec((tm, D_FF), lambda i: (i, 0)),
            row_spec,
            pl.BlockSpec((D_FF, D_MODEL), const2),
            pl.BlockSpec((1, D_MODEL), const2),
            _mod_spec(gt.shape[1], tiles_per_group),
        ],
        out_specs=row_spec,
        compiler_params=pltpu.CompilerParams(
            dimension_semantics=("arbitrary",), vmem_limit_bytes=VMEM_LIMIT),
        name="down_proj",
    )(g2d, x1, w_down_bf16, g_post.reshape(1, D_MODEL), gt)


def _later_mask(n):
    idx = np.arange(n)
    return (idx[None, :] > idx[:, None]).astype(np.float32)


def _alibi_slopes_np():
    return (2.0 ** (-(8.0 / N_DIFF_HEADS) * np.arange(1, N_DIFF_HEADS + 1))).astype(np.float32)


def _alibi_features(slopes2):
    bf = ml_dtypes.bfloat16
    hi = slopes2.astype(bf).astype(np.float32)
    mid = (slopes2 - hi).astype(bf).astype(np.float32)
    lo = (slopes2 - hi - mid).astype(bf).astype(np.float32)
    sfeat = np.zeros((N_DIFF_HEADS, LANES, TQ), np.float32)
    for row, piece in enumerate((hi, mid, lo)):
        sfeat[:, row, :] = piece[:, None]
    kfeat = np.zeros((TQ, LANES), np.float32)
    kfeat[:, :3] = np.arange(TQ, dtype=np.float32)[:, None]
    return (jnp.asarray(sfeat.reshape(N_DIFF_HEADS // 2, 2, LANES, TQ), dtype=BF16),
            jnp.asarray(kfeat, dtype=BF16))


def kernel(x_prompt, x_sample, c_prompt, c_sample, cache_k, cache_v, state_conv, page_table,
           w_ada, b_ada, g_pre_attn, g_post_attn, w_in, w_out, lambda_q1, lambda_k1,
           lambda_q2, lambda_k2, g_subln, g_pre_mlp, g_post_mlp, w_up, conv_w, conv_b, w_down):
    layer = 0
    n_b, seq, _ = x_prompt.shape
    n_dec, dec_t, _ = x_sample.shape
    assert dec_t == DEC_T
    n_pool = cache_k.shape[1]

    w_in_b = w_in[layer].astype(BF16)
    w_out_b = w_out[layer].astype(BF16)
    w_up_b = w_up[layer].astype(BF16)
    w_down_b = w_down[layer].astype(BF16)

    c_all = jnp.concatenate([c_prompt, c_sample], axis=0)
    mod = _modulation(c_all, w_ada[layer], b_ada[layer])
    mod_p = [m.reshape(n_b, 1, D_MODEL) for m in jnp.split(mod[:n_b], N_MOD, axis=-1)]
    mod_s = [jnp.repeat(m, DEC_T, axis=0).reshape(1, n_dec * DEC_T, D_MODEL)
             for m in jnp.split(mod[n_b:], N_MOD, axis=-1)]

    slopes2 = _alibi_slopes_np() * np.float32(LOG2E)
    slope_pairs = jnp.asarray(np.repeat(slopes2, HEAD_DIM).reshape(N_DIFF_HEADS // 2, 1, LANES))
    slope_rep = jnp.asarray(np.broadcast_to(np.repeat(slopes2, 2 * DEC_T)[:, None],
                                            (DF_ROWS, PAGE_SIZE)))
    sfeat, kfeat = _alibi_features(slopes2)
    lam_vecs = jnp.stack([lambda_q1[layer], lambda_k1[layer],
                          lambda_q2[layer], lambda_k2[layer]]).astype(F32)
    g_sub = g_subln[layer].astype(F32)
    g_col = g_sub.reshape(HEAD_DIM, 1)
    g8 = jnp.tile(g_sub, N_DIFF_HEADS).reshape(1, DF_WIDTH)
    usuf = jnp.asarray(_later_mask(TQ), dtype=BF16)
    tri = _later_mask(PAGE_SIZE).T
    trio = jnp.asarray(np.concatenate([tri, np.ones_like(tri)], axis=1), dtype=BF16)

    tm_p = TQ
    tiles_p = seq // tm_p
    xp2d = x_prompt.reshape(n_b * seq, D_MODEL)
    sh_a, sc_a, gt_a, sh_m, sc_m, gt_m = mod_p
    qt_p, kt_p, vt_p, kb_p, vtb_p, kn2_p = _qkv_prompt(xp2d, sc_a, sh_a, g_pre_attn[layer], w_in_b,
                                                       n_b, seq, tm_p)
    kb4 = kb_p.reshape(n_b, tiles_p, TQ, D_MODEL)
    mixed_sb = _sb_prompt_attention(qt_p, kb4, vtb_p, usuf)
    mixed_df = _diff_prompt_attention(qt_p, kb4, vtb_p, slope_pairs, sfeat, kfeat,
                                      _prefix_key_norms(kn2_p, n_b, tiles_p), lam_vecs, g_col)
    tm_r = ROW_TILE
    tiles_r = seq // tm_r
    x1_p, h_p = _attn_out(mixed_sb.reshape(n_b * seq, SB_WIDTH), 0,
                          mixed_df.reshape(n_b * seq, DF_WIDTH), 0,
                          xp2d, w_out_b, g_post_attn[layer], gt_a, g_pre_mlp[layer], sc_m, sh_m,
                          tm_r, tiles_r)
    conv0 = jnp.zeros((n_b, CONV_WIDTH - 1, 2 * D_FF), F32)
    gate_p, conv_p = _up_conv(h_p.reshape(n_b, seq, D_MODEL), conv0, w_up_b,
                              conv_w[layer], conv_b[layer], tm_p)
    y_p = _down_proj(gate_p.reshape(n_b * seq, D_FF), x1_p, w_down_b, g_post_mlp[layer],
                     gt_m, tm_r, tiles_r)

    tm_s = n_dec * DEC_T
    xs2d = x_sample.reshape(tm_s, D_MODEL)
    sh_a, sc_a, gt_a, sh_m, sc_m, gt_m = mod_s
    q_s, k_s, v_s = _qkv_rows(xs2d, sc_a, sh_a, g_pre_attn[layer], w_in_b, tm_s)
    per_seq = lambda a: a.reshape(n_dec, DEC_T, D_MODEL)
    pages = lambda c: c[layer].transpose(0, 2, 3, 1).reshape(n_pool, D_MODEL, PAGE_SIZE)
    mixed_s = _dec_attention(
        page_table, per_seq(q_s), per_seq(k_s), per_seq(v_s), pages(cache_k), pages(cache_v),
        trio, slope_rep, lam_vecs, g8)
    mixed_s2d = mixed_s.reshape(tm_s, D_MODEL)
    x1_s, h_s = _attn_out(mixed_s2d, 0, mixed_s2d, 1, xs2d, w_out_b, g_post_attn[layer], gt_a,
                          g_pre_mlp[layer], sc_m, sh_m, tm_s, 1)
    prev_s = jnp.pad(state_conv[layer].astype(F32),
                     ((0, 0), (0, DEC_T - (CONV_WIDTH - 1)), (0, 0))).reshape(tm_s, 2 * D_FF)
    gate_s, ua_s, ub_s = _up_conv_rows(h_s, prev_s, w_up_b, conv_w[layer], conv_b[layer])
    conv_s = jnp.concatenate([ua_s, ub_s], axis=-1).reshape(n_dec, DEC_T, 2 * D_FF)[:, -2:]
    y_s = _down_proj(gate_s, x1_s, w_down_b, g_post_mlp[layer], gt_m, tm_s, 1)

    heads = lambda a, b, t: a.reshape(1, b, t, N_HEADS, HEAD_DIM)
    heads_t = lambda a: a.reshape(1, n_b, N_HEADS, HEAD_DIM, seq).transpose(0, 1, 4, 2, 3)
    return (y_p.reshape(n_b, seq, D_MODEL),
            y_s.reshape(n_dec, DEC_T, D_MODEL),
            heads_t(kt_p), heads_t(vt_p), conv_p[None],
            heads(k_s, n_dec, DEC_T), heads(v_s, n_dec, DEC_T), conv_s[None])
```
